```python
import math
import jax, jax.numpy as jnp
from jax import lax
import numpy as np

D_MODEL = 1024
BATCH = 16
SEQ = 2048
DEPTH = 2

HEAD_DIM = 64
N_HEADS_A = D_MODEL // (2 * HEAD_DIM)
N_HEADS_B = D_MODEL // (2 * HEAD_DIM)
N_HEADS_DIFF = D_MODEL // (2 * HEAD_DIM)
DILATED_CONFIGS = ((128, 1), (512, 4), (2048, 16))
WINDOW_Q_BLOCK = 128
MOBA_BLOCK = 256
MOBA_TOPK = 3
MOBA_Q_CHUNK = 32
DIFF_Q_BLOCK = 128
ROPE_THETA = 500000.0
ROPE_DIM = HEAD_DIM // 4
D_FF = -(-(8 * D_MODEL) // (3 * 256)) * 256
NORM_EPS = 1e-5
ATTN_SCALE = HEAD_DIM ** -0.5
NEG_INF = -1e30

kernel_name = 'hybrid_dilated_moba_diffattn_block'


def rmsnorm(x, g):
    xf = x.astype(jnp.float32)
    y = xf * lax.rsqrt(jnp.mean(xf * xf, axis=-1, keepdims=True) + NORM_EPS)
    return (y * g.astype(jnp.float32)).astype(x.dtype)


def rope_tables(positions):
    inv_freq = ROPE_THETA ** (-jnp.arange(0, ROPE_DIM, 2, dtype=jnp.float32) / ROPE_DIM)
    ang = positions.astype(jnp.float32)[..., None] * inv_freq
    return jnp.cos(ang), jnp.sin(ang)


def apply_partial_rope(x, cos, sin):
    half = ROPE_DIM // 2
    xf = x.astype(jnp.float32)
    x1, x2 = xf[..., :half], xf[..., half:ROPE_DIM]
    c, s = cos[:, :, None, :], sin[:, :, None, :]
    out = jnp.concatenate([x1 * c - x2 * s, x2 * c + x1 * s, xf[..., ROPE_DIM:]], axis=-1)
    return out.astype(x.dtype)


def strided_window_attention(q, k, v, n_keys):
    B, L, R, H, dh = q.shape
    bq = math.gcd(L, WINDOW_Q_BLOCK)
    nb = L // bq
    kw_len = bq + n_keys
    pad = ((0, 0), (n_keys, 0), (0, 0), (0, 0), (0, 0))
    idx = jnp.arange(nb)[:, None] * bq + jnp.arange(kw_len)[None, :]
    kw = jnp.take(jnp.pad(k, pad), idx, axis=1)
    vw = jnp.take(jnp.pad(v, pad), idx, axis=1)
    qb = q.reshape(B, nb, bq, R, H, dh)
    logits = jnp.einsum('bnqrhd,bnkrhd->bnrhqk', qb, kw).astype(jnp.float32) * ATTN_SCALE
    rel = jnp.arange(bq)[:, None] + n_keys - jnp.arange(kw_len)[None, :]
    band = (rel >= 0) & (rel <= n_keys)
    valid = idx >= n_keys
    mask = band[None] & valid[:, None, :]
    logits = jnp.where(mask[None, :, None, None], logits, NEG_INF)
    m = jnp.max(logits, axis=-1, keepdims=True)
    p = jnp.exp(logits - m)
    denom = jnp.sum(p, axis=-1, keepdims=True)
    o = jnp.einsum('bnrhqk,bnkrhd->bnqrhd', p / denom, vw.astype(jnp.float32))
    lse = (m + jnp.log(denom))[..., 0]
    o = o.reshape(B, L, R, H, dh)
    lse = lse.transpose(0, 1, 4, 2, 3).reshape(B, L, R, H)
    return o, lse


def dilated_attention(q, k, v):
    B, S, H, dh = q.shape
    outs, lses = [], []
    for window, dilation in DILATED_CONFIGS:
        L = S // dilation
        split = lambda t: t.reshape(B, L, dilation, H, dh)
        o, lse = strided_window_attention(split(q), split(k), split(v), window // dilation)
        outs.append(o.reshape(B, S, H, dh))
        lses.append(lse.reshape(B, S, H))
    w = jax.nn.softmax(jnp.stack(lses, axis=-1), axis=-1)
    out = jnp.einsum('bshg,gbshd->bshd', w, jnp.stack(outs, axis=0))
    return out.astype(q.dtype)


def moba_attention(q, k, v):
    B, S, H, dh = q.shape
    nblk = -(-S // MOBA_BLOCK)
    s_pad = nblk * MOBA_BLOCK
    topk = min(MOBA_TOPK, nblk - 1)
    qh = q.transpose(0, 2, 1, 3)
    pad = ((0, 0), (0, 0), (0, s_pad - S), (0, 0))
    kp = jnp.pad(k.transpose(0, 2, 1, 3), pad)
    vp = jnp.pad(v.transpose(0, 2, 1, 3), pad)
    kb = kp.reshape(B, H, nblk, MOBA_BLOCK, dh)
    vb = vp.reshape(B, H, nblk, MOBA_BLOCK, dh)
    kmean = jnp.mean(kb.astype(jnp.float32), axis=3)
    bi = jnp.arange(B)[:, None, None, None]
    hi = jnp.arange(H)[None, :, None, None]

    def chunk_fn(c):
        start = c * MOBA_Q_CHUNK
        qblk = start // MOBA_BLOCK
        qc = lax.dynamic_slice_in_dim(qh, start, MOBA_Q_CHUNK, axis=2)
        ko = lax.dynamic_slice_in_dim(kp, qblk * MOBA_BLOCK, MOBA_BLOCK, axis=2)
        vo = lax.dynamic_slice_in_dim(vp, qblk * MOBA_BLOCK, MOBA_BLOCK, axis=2)
        qpos = start + jnp.arange(MOBA_Q_CHUNK)
        kpos = qblk * MOBA_BLOCK + jnp.arange(MOBA_BLOCK)
        own = jnp.einsum('bhqd,bhkd->bhqk', qc, ko).astype(jnp.float32) * ATTN_SCALE
        own = jnp.where(kpos[None, :] <= qpos[:, None], own, NEG_INF)
        if topk > 0:
            gate = jnp.einsum('bhqd,bhnd->bhqn', qc.astype(jnp.float32), kmean)
            past = jnp.arange(nblk) < qblk
            gate = jnp.where(past, gate, NEG_INF)
            _, gidx = lax.top_k(gate, topk)
            sel_valid = gidx < qblk
            ks = kb[bi, hi, gidx]
            vs = vb[bi, hi, gidx]
            sel = jnp.einsum('bhqd,bhqnkd->bhqnk', qc, ks).astype(jnp.float32) * ATTN_SCALE
            sel = jnp.where(sel_valid[..., None], sel, NEG_INF)
            logits = jnp.concatenate([sel.reshape(B, H, MOBA_Q_CHUNK, topk * MOBA_BLOCK), own], axis=-1)
            p = jax.nn.softmax(logits, axis=-1)
            p_sel = p[..., :topk * MOBA_BLOCK].reshape(B, H, MOBA_Q_CHUNK, topk, MOBA_BLOCK)
            p_own = p[..., topk * MOBA_BLOCK:]
            o = (jnp.einsum('bhqnk,bhqnkd->bhqd', p_sel, vs.astype(jnp.float32))
                 + jnp.einsum('bhqk,bhkd->bhqd', p_own, vo.astype(jnp.float32)))
        else:
            p_own = jax.nn.softmax(own, axis=-1)
            o = jnp.einsum('bhqk,bhkd->bhqd', p_own, vo.astype(jnp.float32))
        return o

    outs = lax.map(chunk_fn, jnp.arange(S // MOBA_Q_CHUNK))
    out = outs.transpose(1, 0, 3, 2, 4).reshape(B, S, H, dh)
    return out.astype(q.dtype)


def hybrid_ab_mixer(xn, w_in, w_out, cos, sin):
    B, S, _ = xn.shape
    proj = xn @ w_in
    wa, wb = N_HEADS_A * HEAD_DIM, N_HEADS_B * HEAD_DIM
    cuts = np.cumsum([wa, wa, wa, wb, wb])
    qa, ka, va, qb, kb, vb = jnp.split(proj, [int(c) for c in cuts], axis=-1)
    heads = lambda t, h: t.reshape(B, S, h, HEAD_DIM)
    qa = apply_partial_rope(heads(qa, N_HEADS_A), cos, sin)
    ka = apply_partial_rope(heads(ka, N_HEADS_A), cos, sin)
    qb = apply_partial_rope(heads(qb, N_HEADS_B), cos, sin)
    kb = apply_partial_rope(heads(kb, N_HEADS_B), cos, sin)
    oa = dilated_attention(qa, ka, heads(va, N_HEADS_A))
    ob = moba_attention(qb, kb, heads(vb, N_HEADS_B))
    o = jnp.concatenate([oa.reshape(B, S, wa), ob.reshape(B, S, wb)], axis=-1)
    return o @ w_out


def lambda_init_fn(layer_idx):
    return 0.8 - 0.6 * math.exp(-0.3 * layer_idx)


def diff_mixer(xn, w_in, w_out, lq1, lk1, lq2, lk2, subln_g, cos, sin, lambda_init):
    B, S, _ = xn.shape
    H = N_HEADS_DIFF
    wq = 2 * H * HEAD_DIM
    q, k, v = jnp.split(xn @ w_in, [wq, 2 * wq], axis=-1)
    q = apply_partial_rope(q.reshape(B, S, 2 * H, HEAD_DIM), cos, sin)
    k = apply_partial_rope(k.reshape(B, S, 2 * H, HEAD_DIM), cos, sin)
    qh = q.reshape(B, S, H, 2, HEAD_DIM).transpose(0, 2, 3, 1, 4)
    kh = k.reshape(B, S, H, 2, HEAD_DIM).transpose(0, 2, 3, 1, 4)
    vh = v.reshape(B, S, H, 2 * HEAD_DIM).transpose(0, 2, 1, 3)
    lam = (jnp.exp(jnp.sum(lq1.astype(jnp.float32) * lk1.astype(jnp.float32)))
           - jnp.exp(jnp.sum(lq2.astype(jnp.float32) * lk2.astype(jnp.float32)))
           + lambda_init)
    kpos = jnp.arange(S)

    def block_fn(c):
        start = c * DIFF_Q_BLOCK
        qc = lax.dynamic_slice_in_dim(qh, start, DIFF_Q_BLOCK, axis=3)
        logits = jnp.einsum('bhiqd,bhikd->bhiqk', qc, kh).astype(jnp.float32) * ATTN_SCALE
        qpos = start + jnp.arange(DIFF_Q_BLOCK)
        logits = jnp.where(kpos[None, :] <= qpos[:, None], logits, NEG_INF)
        p = jax.nn.softmax(logits, axis=-1)
        attn = p[:, :, 0] - lam * p[:, :, 1]
        return jnp.einsum('bhqk,bhkd->bhqd', attn, vh.astype(jnp.float32))

    outs = lax.map(block_fn, jnp.arange(S // DIFF_Q_BLOCK))
    o = outs.transpose(1, 0, 3, 2, 4).reshape(B, S, H, 2 * HEAD_DIM).astype(xn.dtype)
    o = rmsnorm(o, subln_g) * (1.0 - lambda_init)
    return o.reshape(B, S, H * 2 * HEAD_DIM) @ w_out


def swiglu(xn, w_gate, w_up, w_down):
    return (jax.nn.silu(xn @ w_gate) * (xn @ w_up)) @ w_down


def setup_inputs(seed: int = 0) -> dict:
    key = jax.random.key(seed)
    ks = jax.random.split(key, 20)
    n_even = (DEPTH + 1) // 2
    n_odd = DEPTH // 2
    w_ab_in = 3 * (N_HEADS_A + N_HEADS_B) * HEAD_DIM
    w_ab_out = (N_HEADS_A + N_HEADS_B) * HEAD_DIM
    w_diff_in = 3 * N_HEADS_DIFF * 2 * HEAD_DIM
    w_diff_out = N_HEADS_DIFF * 2 * HEAD_DIM
    nrm = lambda k, shape, scale: scale * jax.random.normal(k, shape, jnp.float32)
    gain = lambda k, shape: 1.0 + nrm(k, shape, 0.02)
    offset = jax.random.randint(ks[1], (BATCH, 1), 0, 4096, dtype=jnp.int32)
    positions = (offset + jnp.arange(SEQ, dtype=jnp.int32)[None, :]).astype(jnp.int32)
    return {
        'x': nrm(ks[0], (BATCH, SEQ, D_MODEL), 1.0),
        'positions': positions,
        'ab_norm_g': gain(ks[2], (n_even, D_MODEL)),
        'ab_w_in': nrm(ks[3], (n_even, D_MODEL, w_ab_in), D_MODEL ** -0.5),
        'ab_w_out': nrm(ks[4], (n_even, w_ab_out, D_MODEL), w_ab_out ** -0.5),
        'diff_norm_g': gain(ks[5], (n_odd, D_MODEL)),
        'diff_w_in': nrm(ks[6], (n_odd, D_MODEL, w_diff_in), D_MODEL ** -0.5),
        'diff_w_out': nrm(ks[7], (n_odd, w_diff_out, D_MODEL), w_diff_out ** -0.5),
        'diff_lambda_q1': nrm(ks[8], (n_odd, HEAD_DIM), 0.1),
        'diff_lambda_k1': nrm(ks[9], (n_odd, HEAD_DIM), 0.1),
        'diff_lambda_q2': nrm(ks[10], (n_odd, HEAD_DIM), 0.1),
        'diff_lambda_k2': nrm(ks[11], (n_odd, HEAD_DIM), 0.1),
        'diff_subln_g': gain(ks[12], (n_odd, 2 * HEAD_DIM)),
        'ffn_norm_g': gain(ks[13], (DEPTH, D_MODEL)),
        'ffn_w_gate': nrm(ks[14], (DEPTH, D_MODEL, D_FF), D_MODEL ** -0.5),
        'ffn_w_up': nrm(ks[15], (DEPTH, D_MODEL, D_FF), D_MODEL ** -0.5),
        'ffn_w_down': nrm(ks[16], (DEPTH, D_FF, D_MODEL), D_FF ** -0.5),
        'final_norm_g': gain(ks[17], (D_MODEL,)),
    }


def reference(x, positions, ab_norm_g, ab_w_in, ab_w_out, diff_norm_g, diff_w_in, diff_w_out,
              diff_lambda_q1, diff_lambda_k1, diff_lambda_q2, diff_lambda_k2, diff_subln_g,
              ffn_norm_g, ffn_w_gate, ffn_w_up, ffn_w_down, final_norm_g):
    cos, sin = rope_tables(positions)
    h = x
    for layer in range(DEPTH):
        i = layer // 2
        if layer % 2 == 0:
            h = h + hybrid_ab_mixer(rmsnorm(h, ab_norm_g[i]), ab_w_in[i], ab_w_out[i], cos, sin)
        else:
            h = h + diff_mixer(rmsnorm(h, diff_norm_g[i]), diff_w_in[i], diff_w_out[i],
                               diff_lambda_q1[i], diff_lambda_k1[i], diff_lambda_q2[i],
                               diff_lambda_k2[i], diff_subln_g[i], cos, sin,
                               lambda_init_fn(layer))
        h = h + swiglu(rmsnorm(h, ffn_norm_g[layer]), ffn_w_gate[layer], ffn_w_up[layer], ffn_w_down[layer])
    return rmsnorm(h, final_norm_g)
```

```python
import functools
import math

import numpy as np
import jax
import jax.numpy as jnp
from jax import lax
from jax.experimental import pallas as pl
from jax.experimental.pallas import tpu as pltpu

D_MODEL = 1024
HEAD_DIM = 64
LANES = 128
ROPE_DIM = HEAD_DIM // 4
ROPE_HALF = ROPE_DIM // 2
ROPE_THETA = 500000.0
D_FF = 2816
NORM_EPS = 1e-5
ATTN_SCALE = HEAD_DIM ** -0.5
NEG = -1e30
DILATED_CONFIGS = ((128, 1), (512, 4), (2048, 16))
MOBA_BLOCK = 256
MOBA_TOPK = 3

ATT_TILE = 256
ROW_TILE = 512
VMEM_LIMIT = 56 * 1024 * 1024

F32 = jnp.float32
BF16 = jnp.bfloat16
_NT = (((1,), (1,)), ((), ()))


def _cparams(sem):
    return pltpu.CompilerParams(dimension_semantics=sem, vmem_limit_bytes=VMEM_LIMIT)


def _rope_table_kernel(pos_ref, invf_ref, c_ref, s1_ref, s2_ref):
    ang = pos_ref[...].astype(F32) * invf_ref[...]
    lane = lax.broadcasted_iota(jnp.int32, ang.shape, 1) % HEAD_DIM
    cos, sin = jnp.cos(ang), jnp.sin(ang)
    c_ref[...] = jnp.where(lane < ROPE_DIM, cos, 1.0)
    s1_ref[...] = jnp.where(lane < ROPE_HALF, -sin, 0.0)
    s2_ref[...] = jnp.where((lane >= ROPE_HALF) & (lane < ROPE_DIM), sin, 0.0)


def _rope_tables(positions):
    t = positions.size
    tm = 1024
    inv_freq = ROPE_THETA ** (-jnp.arange(0, ROPE_DIM, 2, dtype=F32) / ROPE_DIM)
    lane = np.arange(LANES) % HEAD_DIM
    invf = jnp.where(lane < ROPE_DIM, inv_freq[lane % ROPE_HALF], 0.0).reshape(1, LANES)
    tab = jax.ShapeDtypeStruct((t, LANES), F32)
    return pl.pallas_call(
        _rope_table_kernel,
        grid=(t // tm,),
        in_specs=[pl.BlockSpec((tm, 1), lambda i: (i, 0)),
                  pl.BlockSpec((1, LANES), lambda i: (0, 0))],
        out_specs=[pl.BlockSpec((tm, LANES), lambda i: (i, 0))] * 3,
        out_shape=[tab] * 3,
        compiler_params=_cparams(("parallel",)),
        name="rope_tables",
    )(positions.reshape(t, 1), invf)


def _norm_proj_kernel(x_ref, g_ref, w_ref, c_ref, s1_ref, s2_ref, o_ref, *, rope_tiles, chunk):
    x = x_ref[...]
    ms = jnp.mean(x * x, axis=-1, keepdims=True)
    xn = (x * lax.rsqrt(ms + NORM_EPS) * g_ref[...]).astype(BF16)
    c, s1, s2 = c_ref[...], s1_ref[...], s2_ref[...]
    n_out = o_ref.shape[1]
    for n in range(n_out // chunk):
        acc = jnp.dot(xn, w_ref[:, n * chunk:(n + 1) * chunk], preferred_element_type=F32)
        for t in range(chunk // LANES):
            col = n * chunk + t * LANES
            seg = acc[:, t * LANES:(t + 1) * LANES]
            if col // LANES in rope_tiles:
                seg = (seg * c + pltpu.roll(seg, LANES - ROPE_HALF, 1) * s1
                       + pltpu.roll(seg, ROPE_HALF, 1) * s2)
            o_ref[:, col:col + LANES] = seg.astype(BF16)


def _norm_proj(x, g, w, tabs, rope_tiles):
    t, d = x.shape
    n_out = w.shape[1]
    tm = ROW_TILE
    kern = functools.partial(_norm_proj_kernel, rope_tiles=frozenset(rope_tiles), chunk=512)
    row = lambda i: (i, 0)
    fixed = lambda i: (0, 0)
    return pl.pallas_call(
        kern,
        grid=(t // tm,),
        in_specs=[pl.BlockSpec((tm, d), row), pl.BlockSpec((1, d), fixed),
                  pl.BlockSpec((d, n_out), fixed),
                  pl.BlockSpec((tm, LANES), row), pl.BlockSpec((tm, LANES), row),
                  pl.BlockSpec((tm, LANES), row)],
        out_specs=pl.BlockSpec((tm, n_out), row),
        out_shape=jax.ShapeDtypeStruct((t, n_out), BF16),
        compiler_params=_cparams(("parallel",)),
        name="norm_proj_rope",
    )(x, g.reshape(1, d), w, *tabs)


def _head_masks(shape):
    lane = lax.broadcasted_iota(jnp.int32, shape, 1)
    return lane < HEAD_DIM, lane >= HEAD_DIM


def _softmax_step(s, v, state):
    m, l, acc = state
    m_new = jnp.maximum(m, jnp.max(s, axis=-1, keepdims=True))
    alpha = jnp.exp(m - m_new)
    p = jnp.exp(s - m_new)
    l = alpha * l + jnp.sum(p, axis=-1, keepdims=True)
    acc = alpha * acc + jnp.dot(p.astype(BF16), v, preferred_element_type=F32)
    return m_new, l, acc


def _first_step(s, v):
    m = jnp.max(s, axis=-1, keepdims=True)
    p = jnp.exp(s - m)
    return m, jnp.sum(p, axis=-1, keepdims=True), jnp.dot(p.astype(BF16), v, preferred_element_type=F32)


def _causal_bias(t):
    r = lax.broadcasted_iota(jnp.int32, (t, t), 0)
    c = lax.broadcasted_iota(jnp.int32, (t, t), 1)
    return jnp.where(c <= r, 0.0, NEG).astype(F32)


def _kv_block(ref, j, t):
    rows = pl.ds(pl.multiple_of(j * t, t), t)
    return ref[rows, :] if len(ref.shape) == 2 else ref[0, rows, :]


def _dilated_bias_tables(t):
    windows = sorted(w for w, _ in DILATED_CONFIGS)
    far = windows[-2] // t + 1
    tabs = []
    for delta in range(far + 1):
        d = delta * t + np.arange(t)[:, None] - np.arange(t)[None, :]
        mult = np.zeros((t, t), np.float64)
        for window, dil in DILATED_CONFIGS:
            mult += (d >= 0) & (d % dil == 0) & (d <= window)
        tabs.append(np.where(mult > 0, np.log(np.maximum(mult, 1.0)), NEG))
    return jnp.asarray(np.stack(tabs), F32)


def _dilated_kernel(q_ref, k_ref, v_ref, bias_ref, o_ref):
    t = ATT_TILE
    qi = pl.program_id(2)
    n_tab = bias_ref.shape[0]
    q = q_ref[0]
    lo, hi = _head_masks(q.shape)
    outs = []
    for hm in (lo, hi):
        qh = jnp.where(hm, q.astype(F32) * ATTN_SCALE, 0.0).astype(BF16)
        s = lax.dot_general(qh, _kv_block(k_ref, qi, t), _NT, preferred_element_type=F32)
        state = _first_step(s + bias_ref[0], _kv_block(v_ref, qi, t))

        def body(j, state, qh=qh):
            s = lax.dot_general(qh, _kv_block(k_ref, j, t), _NT, preferred_element_type=F32)
            s = s + bias_ref[jnp.minimum(qi - j, n_tab - 1)]
            return _softmax_step(s, _kv_block(v_ref, j, t), state)

        _, l, acc = lax.fori_loop(0, qi, body, state)
        outs.append(acc / l)
    o_ref[0] = jnp.where(lo, outs[0], outs[1]).astype(BF16)


def _dilated_attention(qkv, q_tile0, k_tile0, v_tile0, n_pairs):
    b, s, _ = qkv.shape
    t = ATT_TILE
    bias = _dilated_bias_tables(t)
    return pl.pallas_call(
        _dilated_kernel,
        grid=(b, n_pairs, s // t),
        in_specs=[pl.BlockSpec((1, t, LANES), lambda b_, p, i: (b_, i, q_tile0 + p)),
                  pl.BlockSpec((1, s, LANES), lambda b_, p, i: (b_, 0, k_tile0 + p)),
                  pl.BlockSpec((1, s, LANES), lambda b_, p, i: (b_, 0, v_tile0 + p)),
                  pl.BlockSpec(bias.shape, lambda b_, p, i: (0, 0, 0))],
        out_specs=pl.BlockSpec((1, t, LANES), lambda b_, p, i: (b_, i, p)),
        out_shape=jax.ShapeDtypeStruct((b, s, n_pairs * LANES), BF16),
        compiler_params=_cparams(("parallel", "parallel", "arbitrary")),
        name="dilated_attention",
    )(qkv, qkv, qkv, bias)


def _moba_kernel(q_ref, k_ref, v_ref, o_ref, ka_ref, kb_ref, km_ref):
    t = MOBA_BLOCK
    s_len = k_ref.shape[1]
    nblk = s_len // t
    qi = pl.program_id(2)

    @pl.when(qi == 0)
    def _prepare():
        k = k_ref[0].astype(F32)
        lane = lax.broadcasted_iota(jnp.int32, k.shape, 1)
        blk = lax.broadcasted_iota(jnp.int32, k.shape, 0) // t
        ka_ref[...] = jnp.where(lane < HEAD_DIM, k, jnp.where(lane - HEAD_DIM == blk, 1.0, 0.0)).astype(BF16)
        kb_ref[...] = jnp.where(lane >= HEAD_DIM, k, jnp.where(lane == blk, 1.0, 0.0)).astype(BF16)
        row = lax.broadcasted_iota(jnp.int32, (LANES, LANES), 0)
        rpad = jnp.zeros((LANES, LANES), F32)
        for n in range(nblk):
            mean_n = jnp.mean(k_ref[0, n * t:(n + 1) * t, :].astype(F32), axis=0, keepdims=True)
            rpad = jnp.where(row == n, mean_n, rpad)
        rt = rpad.T
        km_ref[...] = jnp.where(row < HEAD_DIM, pltpu.roll(rt, HEAD_DIM, 1), rt)

    q = q_ref[0]
    lo, hi = _head_masks(q.shape)
    lane = lax.broadcasted_iota(jnp.int32, q.shape, 1)
    lane_f = lane.astype(F32)
    qf = q.astype(F32)
    gate_all = jnp.dot(qf, km_ref[...], preferred_element_type=F32,
                       precision=lax.Precision.HIGHEST)
    causal = _causal_bias(t)
    outs = []
    for hm, kaug_ref, base in ((lo, ka_ref, HEAD_DIM), (hi, kb_ref, 0)):
        past = (lane >= base) & (lane < base + qi)
        g = jnp.where(past, gate_all, -jnp.inf)
        keep = jnp.zeros(q.shape, jnp.bool_)
        for _ in range(MOBA_TOPK):
            mx = jnp.max(g, axis=-1, keepdims=True)
            first = jnp.min(jnp.where(g == mx, lane_f, 1e9), axis=-1, keepdims=True)
            hit = lane_f == first
            keep = keep | hit
            g = jnp.where(hit, -jnp.inf, g)
        drop = jnp.where(past & jnp.logical_not(keep), NEG, 0.0)
        qh = jnp.where(hm, qf * ATTN_SCALE, drop).astype(BF16)

        s = lax.dot_general(qh, _kv_block(kaug_ref, qi, t), _NT, preferred_element_type=F32)
        state = _first_step(s + causal, _kv_block(v_ref, qi, t))

        def body(j, state, qh=qh, kaug_ref=kaug_ref):
            s = lax.dot_general(qh, _kv_block(kaug_ref, j, t), _NT, preferred_element_type=F32)
            return _softmax_step(s, _kv_block(v_ref, j, t), state)

        _, l, acc = lax.fori_loop(0, qi, body, state)
        outs.append(acc / l)
    o_ref[0] = jnp.where(lo, outs[0], outs[1]).astype(BF16)


def _moba_attention(qkv, q_tile0, k_tile0, v_tile0, n_pairs):
    b, s, _ = qkv.shape
    t = MOBA_BLOCK
    return pl.pallas_call(
        _moba_kernel,
        grid=(b, n_pairs, s // t),
        in_specs=[pl.BlockSpec((1, t, LANES), lambda b_, p, i: (b_, i, q_tile0 + p)),
                  pl.BlockSpec((1, s, LANES), lambda b_, p, i: (b_, 0, k_tile0 + p)),
                  pl.BlockSpec((1, s, LANES), lambda b_, p, i: (b_, 0, v_tile0 + p))],
        out_specs=pl.BlockSpec((1, t, LANES), lambda b_, p, i: (b_, i, p)),
        out_shape=jax.ShapeDtypeStruct((b, s, n_pairs * LANES), BF16),
        scratch_shapes=[pltpu.VMEM((s, LANES), BF16), pltpu.VMEM((s, LANES), BF16),
                        pltpu.VMEM((LANES, LANES), F32)],
        compiler_params=_cparams(("parallel", "parallel", "arbitrary")),
        name="moba_attention",
    )(qkv, qkv, qkv)


def _diff_kernel(q_ref, k_ref, v_ref, lq1_ref, lk1_ref, lq2_ref, lk2_ref, g_ref, o_ref, *, lambda_init):
    t = ATT_TILE
    qi = pl.program_id(2)
    q = q_ref[0]
    lo, hi = _head_masks(q.shape)
    causal = _causal_bias(t)
    outs = []
    for hm in (lo, hi):
        qh = jnp.where(hm, q.astype(F32) * ATTN_SCALE, 0.0).astype(BF16)
        s = lax.dot_general(qh, _kv_block(k_ref, qi, t), _NT, preferred_element_type=F32)
        state = _first_step(s + causal, _kv_block(v_ref, qi, t))

        def body(j, state, qh=qh):
            s = lax.dot_general(qh, _kv_block(k_ref, j, t), _NT, preferred_element_type=F32)
            return _softmax_step(s, _kv_block(v_ref, j, t), state)

        _, l, acc = lax.fori_loop(0, qi, body, state)
        outs.append(acc / l)
    lam = (jnp.exp(jnp.sum(lq1_ref[...] * lk1_ref[...], axis=-1, keepdims=True))
           - jnp.exp(jnp.sum(lq2_ref[...] * lk2_ref[...], axis=-1, keepdims=True)) + lambda_init)
    o = outs[0] - lam * outs[1]
    y = o * lax.rsqrt(jnp.mean(o * o, axis=-1, keepdims=True) + NORM_EPS)
    o_ref[0] = ((y * g_ref[...]) * (1.0 - lambda_init)).astype(BF16)


def _diff_attention(qkv, lq1, lk1, lq2, lk2, subln_g, lambda_init, n_heads):
    b, s, _ = qkv.shape
    t = ATT_TILE
    vec = lambda a: a.reshape(1, -1).astype(F32)
    small = lambda n: pl.BlockSpec((1, n), lambda b_, h, i: (0, 0))
    return pl.pallas_call(
        functools.partial(_diff_kernel, lambda_init=lambda_init),
        grid=(b, n_heads, s // t),
        in_specs=[pl.BlockSpec((1, t, LANES), lambda b_, h, i: (b_, i, h)),
                  pl.BlockSpec((1, s, LANES), lambda b_, h, i: (b_, 0, n_heads + h)),
                  pl.BlockSpec((1, s, LANES), lambda b_, h, i: (b_, 0, 2 * n_heads + h)),
                  small(HEAD_DIM), small(HEAD_DIM), small(HEAD_DIM), small(HEAD_DIM), small(LANES)],
        out_specs=pl.BlockSpec((1, t, LANES), lambda b_, h, i: (b_, i, h)),
        out_shape=jax.ShapeDtypeStruct((b, s, n_heads * LANES), BF16),
        compiler_params=_cparams(("parallel", "parallel", "arbitrary")),
        name="diff_attention",
    )(qkv, qkv, qkv, vec(lq1), vec(lk1), vec(lq2), vec(lk2), vec(subln_g))


def _out_proj_kernel(*refs):
    h_ref, o_ref = refs[0], refs[-1]
    acts_ws = refs[1:-1]
    acc = h_ref[...]
    for a_ref, w_ref in zip(acts_ws[0::2], acts_ws[1::2]):
        acc = acc + jnp.dot(a_ref[...], w_ref[...], preferred_element_type=F32)
    o_ref[...] = acc


def _out_proj(h, acts, ws):
    t, d = h.shape
    tm = ROW_TILE
    row = lambda i: (i, 0)
    fixed = lambda i: (0, 0)
    in_specs = [pl.BlockSpec((tm, d), row)]
    args = [h]
    for a, w in zip(acts, ws):
        in_specs += [pl.BlockSpec((tm, a.shape[1]), row), pl.BlockSpec(w.shape, fixed)]
        args += [a, w]
    return pl.pallas_call(
        _out_proj_kernel,
        grid=(t // tm,),
        in_specs=in_specs,
        out_specs=pl.BlockSpec((tm, d), row),
        out_shape=jax.ShapeDtypeStruct((t, d), F32),
        compiler_params=_cparams(("parallel",)),
        name="out_proj_residual",
    )(*args)


def _ffn_kernel(h_ref, g_ref, wg_ref, wu_ref, wd_ref, gf_ref, o_ref, *, chunk, final_norm):
    x = h_ref[...]
    ms = jnp.mean(x * x, axis=-1, keepdims=True)
    xn = (x * lax.rsqrt(ms + NORM_EPS) * g_ref[...]).astype(BF16)
    acc = x
    for c in range(wg_ref.shape[1] // chunk):
        sl = slice(c * chunk, (c + 1) * chunk)
        gate = jnp.dot(xn, wg_ref[:, sl], preferred_element_type=F32)
        up = jnp.dot(xn, wu_ref[:, sl], preferred_element_type=F32)
        mid = (gate * jax.nn.sigmoid(gate) * up).astype(BF16)
        acc = acc + jnp.dot(mid, wd_ref[sl, :], preferred_element_type=F32)
    if final_norm:
        ms = jnp.mean(acc * acc, axis=-1, keepdims=True)
        acc = acc * lax.rsqrt(ms + NORM_EPS) * gf_ref[...]
    o_ref[...] = acc


def _ffn(h, g, wg, wu, wd, g_final, final_norm):
    t, d = h.shape
    ff = wg.shape[1]
    tm = ROW_TILE
    row = lambda i: (i, 0)
    fixed = lambda i: (0, 0)
    once = dict(pipeline_mode=pl.Buffered(1))
    return pl.pallas_call(
        functools.partial(_ffn_kernel, chunk=256, final_norm=final_norm),
        grid=(t // tm,),
        in_specs=[pl.BlockSpec((tm, d), row), pl.BlockSpec((1, d), fixed),
                  pl.BlockSpec((d, ff), fixed, **once), pl.BlockSpec((d, ff), fixed, **once),
                  pl.BlockSpec((ff, d), fixed, **once), pl.BlockSpec((1, d), fixed)],
        out_specs=pl.BlockSpec((tm, d), row),
        out_shape=jax.ShapeDtypeStruct((t, d), F32),
        compiler_params=_cparams(("parallel",)),
        name="ffn_swiglu",
    )(h, g.reshape(1, d), wg, wu, wd, g_final.reshape(1, d))


def _lambda_init(layer_idx):
    return 0.8 - 0.6 * math.exp(-0.3 * layer_idx)


def kernel(x, positions, ab_norm_g, ab_w_in, ab_w_out, diff_norm_g, diff_w_in, diff_w_out, diff_lambda_q1, diff_lambda_k1, diff_lambda_q2, diff_lambda_k2, diff_subln_g, ffn_norm_g, ffn_w_gate, ffn_w_up, ffn_w_down, final_norm_g):
    b, s, d = x.shape
    t = b * s
    n_pairs = d // (4 * HEAD_DIM)
    n_diff = d // (2 * HEAD_DIM)
    tabs = _rope_tables(positions)
    h = x.reshape(t, d)

    rope0 = [c for c in range(6 * n_pairs) if (c // n_pairs) in (0, 1, 3, 4)]
    qkv = _norm_proj(h, ab_norm_g[0], ab_w_in[0].astype(BF16), tabs, rope0).reshape(b, s, -1)
    oa = _dilated_attention(qkv, 0, n_pairs, 2 * n_pairs, n_pairs)
    ob = _moba_attention(qkv, 3 * n_pairs, 4 * n_pairs, 5 * n_pairs, n_pairs)
    w_out = ab_w_out[0].astype(BF16)
    wa = n_pairs * LANES
    h = _out_proj(h, [oa.reshape(t, wa), ob.reshape(t, wa)], [w_out[:wa], w_out[wa:]])
    h = _ffn(h, ffn_norm_g[0], ffn_w_gate[0].astype(BF16), ffn_w_up[0].astype(BF16),
             ffn_w_down[0].astype(BF16), final_norm_g, False)

    rope1 = list(range(2 * n_diff))
    qkv = _norm_proj(h, diff_norm_g[0], diff_w_in[0].astype(BF16), tabs, rope1).reshape(b, s, -1)
    od = _diff_attention(qkv, diff_lambda_q1[0], diff_lambda_k1[0], diff_lambda_q2[0], diff_lambda_k2[0],
                         diff_subln_g[0], _lambda_init(1), n_diff)
    h = _out_proj(h, [od.reshape(t, d)], [diff_w_out[0].astype(BF16)])
    h = _ffn(h, ffn_norm_g[1], ffn_w_gate[1].astype(BF16), ffn_w_up[1].astype(BF16),
             ffn_w_down[1].astype(BF16), final_norm_g, True)
    return h.reshape(b, s, d)
```

```python
import functools
import math

import numpy as np
import jax
import jax.numpy as jnp
from jax import lax
from jax.experimental import pallas as pl
from jax.experimental.pallas import tpu as pltpu

D_MODEL = 1024
HEAD_DIM = 64
LANES = 128
ROPE_DIM = HEAD_DIM // 4
ROPE_HALF = ROPE_DIM // 2
ROPE_THETA = 500000.0
D_FF = 2816
NORM_EPS = 1e-5
ATTN_SCALE = HEAD_DIM ** -0.5
NEG = -1e30
DILATED_CONFIGS = ((128, 1), (512, 4), (2048, 16))
MOBA_BLOCK = 256
MOBA_TOPK = 3

ATT_TILE = 256
ROW_TILE = 512
VMEM_LIMIT = 56 * 1024 * 1024

F32 = jnp.float32
BF16 = jnp.bfloat16
_NT = (((1,), (1,)), ((), ()))


def _cparams(sem):
    return pltpu.CompilerParams(dimension_semantics=sem, vmem_limit_bytes=VMEM_LIMIT)


def _rope_table_kernel(pos_ref, invf_ref, c_ref, s1_ref, s2_ref):
    ang = pos_ref[...].astype(F32) * invf_ref[...]
    lane = lax.broadcasted_iota(jnp.int32, ang.shape, 1) % HEAD_DIM
    cos, sin = jnp.cos(ang), jnp.sin(ang)
    c_ref[...] = jnp.where(lane < ROPE_DIM, cos, 1.0)
    s1_ref[...] = jnp.where(lane < ROPE_HALF, -sin, 0.0)
    s2_ref[...] = jnp.where((lane >= ROPE_HALF) & (lane < ROPE_DIM), sin, 0.0)


def _rope_tables(positions):
    t = positions.size
    tm = 1024
    inv_freq = ROPE_THETA ** (-jnp.arange(0, ROPE_DIM, 2, dtype=F32) / ROPE_DIM)
    lane = np.arange(LANES) % HEAD_DIM
    invf = jnp.where(lane < ROPE_DIM, inv_freq[lane % ROPE_HALF], 0.0).reshape(1, LANES)
    tab = jax.ShapeDtypeStruct((t, LANES), F32)
    return pl.pallas_call(
        _rope_table_kernel,
        grid=(t // tm,),
        in_specs=[pl.BlockSpec((tm, 1), lambda i: (i, 0)),
                  pl.BlockSpec((1, LANES), lambda i: (0, 0))],
        out_specs=[pl.BlockSpec((tm, LANES), lambda i: (i, 0))] * 3,
        out_shape=[tab] * 3,
        compiler_params=_cparams(("parallel",)),
        name="rope_tables",
    )(positions.reshape(t, 1), invf)


def _norm_proj_kernel(x_ref, g_ref, w_ref, c_ref, s1_ref, s2_ref, o_ref, *, rope_tiles, chunk):
    x = x_ref[...]
    ms = jnp.mean(x * x, axis=-1, keepdims=True)
    xn = (x * lax.rsqrt(ms + NORM_EPS) * g_ref[...]).astype(BF16)
    c, s1, s2 = c_ref[...], s1_ref[...], s2_ref[...]
    n_out = o_ref.shape[1]
    for n in range(n_out // chunk):
        acc = jnp.dot(xn, w_ref[:, n * chunk:(n + 1) * chunk], preferred_element_type=F32)
        for t in range(chunk // LANES):
            col = n * chunk + t * LANES
            seg = acc[:, t * LANES:(t + 1) * LANES]
            if col // LANES in rope_tiles:
                seg = (seg * c + pltpu.roll(seg, LANES - ROPE_HALF, 1) * s1
                       + pltpu.roll(seg, ROPE_HALF, 1) * s2)
            o_ref[:, col:col + LANES] = seg.astype(BF16)


def _norm_proj(x, g, w, tabs, rope_tiles):
    t, d = x.shape
    n_out = w.shape[1]
    tm = ROW_TILE
    kern = functools.partial(_norm_proj_kernel, rope_tiles=frozenset(rope_tiles), chunk=512)
    row = lambda i: (i, 0)
    fixed = lambda i: (0, 0)
    return pl.pallas_call(
        kern,
        grid=(t // tm,),
        in_specs=[pl.BlockSpec((tm, d), row), pl.BlockSpec((1, d), fixed),
                  pl.BlockSpec((d, n_out), fixed),
                  pl.BlockSpec((tm, LANES), row), pl.BlockSpec((tm, LANES), row),
                  pl.BlockSpec((tm, LANES), row)],
        out_specs=pl.BlockSpec((tm, n_out), row),
        out_shape=jax.ShapeDtypeStruct((t, n_out), BF16),
        compiler_params=_cparams(("parallel",)),
        name="norm_proj_rope",
    )(x, g.reshape(1, d), w, *tabs)


def _head_masks(shape):
    lane = lax.broadcasted_iota(jnp.int32, shape, 1)
    return lane < HEAD_DIM, lane >= HEAD_DIM


def _causal_table(t):
    d = np.arange(t)[:, None] - np.arange(t)[None, :]
    return jnp.asarray(np.where(d >= 0, 0.0, NEG), F32)


def _attend(qh, k_of, v_of, bias_of, n_blocks, s_ref, p_ref):
    t = ATT_TILE
    mrun = None
    for j in range(n_blocks):
        s = lax.dot_general(qh, k_of(j), _NT, preferred_element_type=F32)
        bias = bias_of(j)
        if bias is not None:
            s = s + bias
        s_ref[:, j * t:(j + 1) * t] = s
        mj = jnp.maximum(s[:, :LANES], s[:, LANES:])
        mrun = mj if mrun is None else jnp.maximum(mrun, mj)
    m = jnp.max(mrun, axis=-1, keepdims=True)
    lrun = None
    for j in range(n_blocks):
        p = jnp.exp(s_ref[:, j * t:(j + 1) * t] - m)
        p_ref[:, j * t:(j + 1) * t] = p.astype(BF16)
        lj = p[:, :LANES] + p[:, LANES:]
        lrun = lj if lrun is None else lrun + lj
    l = jnp.sum(lrun, axis=-1, keepdims=True)
    n = n_blocks * t
    acc = jnp.dot(p_ref[:, :n], v_of(slice(0, n)), preferred_element_type=F32)
    return acc / l


def _blocks_of(ref):
    t = ATT_TILE

    def get(j):
        rows = j if isinstance(j, slice) else slice(j * t, (j + 1) * t)
        return ref[rows, :] if len(ref.shape) == 2 else ref[0, rows, :]
    return get


def _dilated_bias_tables(t):
    windows = sorted(w for w, _ in DILATED_CONFIGS)
    far = windows[-2] // t + 1
    tabs = []
    for delta in range(far + 1):
        d = delta * t + np.arange(t)[:, None] - np.arange(t)[None, :]
        mult = np.zeros((t, t), np.float64)
        for window, dil in DILATED_CONFIGS:
            mult += (d >= 0) & (d % dil == 0) & (d <= window)
        tabs.append(np.where(mult > 0, np.log(np.maximum(mult, 1.0)), NEG))
    return jnp.asarray(np.stack(tabs), F32)


def _dilated_kernel(q_ref, k_ref, v_ref, bias_ref, o_ref, s_refs, p_refs):
    t = ATT_TILE
    n_tab = bias_ref.shape[0]
    k_of, v_of = _blocks_of(k_ref), _blocks_of(v_ref)
    for i in range(q_ref.shape[1] // t):
        q = q_ref[0, i * t:(i + 1) * t, :].astype(F32)
        lo, hi = _head_masks(q.shape)
        outs = []
        for h, hm in enumerate((lo, hi)):
            qh = jnp.where(hm, q * ATTN_SCALE, 0.0).astype(BF16)
            bias_of = lambda j, i=i: bias_ref[min(i - j, n_tab - 1)]
            outs.append(_attend(qh, k_of, v_of, bias_of, i + 1, s_refs.at[h], p_refs.at[h]))
        o_ref[0, i * t:(i + 1) * t, :] = jnp.where(lo, outs[0], outs[1]).astype(BF16)


def _attn_scratch(s):
    return [pltpu.VMEM((2, ATT_TILE, s), F32), pltpu.VMEM((2, ATT_TILE, s), BF16)]


def _dilated_attention(qkv, q_tile0, k_tile0, v_tile0, n_pairs):
    b, s, _ = qkv.shape
    bias = _dilated_bias_tables(ATT_TILE)
    col = lambda c0: pl.BlockSpec((1, s, LANES), lambda b_, p: (b_, 0, c0 + p))
    return pl.pallas_call(
        _dilated_kernel,
        grid=(b, n_pairs),
        in_specs=[col(q_tile0), col(k_tile0), col(v_tile0),
                  pl.BlockSpec(bias.shape, lambda b_, p: (0, 0, 0))],
        out_specs=col(0),
        out_shape=jax.ShapeDtypeStruct((b, s, n_pairs * LANES), BF16),
        scratch_shapes=_attn_scratch(s),
        compiler_params=_cparams(("parallel", "parallel")),
        name="dilated_attention",
    )(qkv, qkv, qkv, bias)


def _moba_kernel(q_ref, k_ref, v_ref, causal_ref, o_ref, ka_ref, kb_ref, km_ref, s_refs, p_refs):
    t = MOBA_BLOCK
    s_len = k_ref.shape[1]
    nblk = s_len // t

    k = k_ref[0].astype(F32)
    lane_k = lax.broadcasted_iota(jnp.int32, k.shape, 1)
    blk = lax.broadcasted_iota(jnp.int32, k.shape, 0) // t
    ka_ref[...] = jnp.where(lane_k < HEAD_DIM, k, jnp.where(lane_k - HEAD_DIM == blk, 1.0, 0.0)).astype(BF16)
    kb_ref[...] = jnp.where(lane_k >= HEAD_DIM, k, jnp.where(lane_k == blk, 1.0, 0.0)).astype(BF16)
    row = lax.broadcasted_iota(jnp.int32, (LANES, LANES), 0)
    rpad = jnp.zeros((LANES, LANES), F32)
    for n in range(nblk):
        mean_n = jnp.mean(k_ref[0, n * t:(n + 1) * t, :].astype(F32), axis=0, keepdims=True)
        rpad = jnp.where(row == n, mean_n, rpad)
    rt = rpad.T
    km_ref[...] = jnp.where(row < HEAD_DIM, pltpu.roll(rt, HEAD_DIM, 1), rt)

    v_of = _blocks_of(v_ref)
    for i in range(nblk):
        q = q_ref[0, i * t:(i + 1) * t, :].astype(F32)
        lo, hi = _head_masks(q.shape)
        lane = lax.broadcasted_iota(jnp.int32, q.shape, 1)
        select = i > MOBA_TOPK
        if select:
            lane_f = lane.astype(F32)
            gate_all = jnp.dot(q, km_ref[...], preferred_element_type=F32, precision=lax.Precision.HIGHEST)
        outs = []
        for h, (hm, kaug_ref, base) in enumerate(((lo, ka_ref, HEAD_DIM), (hi, kb_ref, 0))):
            drop = 0.0
            if select:
                past = (lane >= base) & (lane < base + i)
                g = jnp.where(past, gate_all, -jnp.inf)
                keep = jnp.zeros(q.shape, jnp.bool_)
                for _ in range(MOBA_TOPK):
                    mx = jnp.max(g, axis=-1, keepdims=True)
                    first = jnp.min(jnp.where(g == mx, lane_f, 1e9), axis=-1, keepdims=True)
                    hit = lane_f == first
                    keep = keep | hit
                    g = jnp.where(hit, -jnp.inf, g)
                drop = jnp.where(past & jnp.logical_not(keep), NEG, 0.0)
            qh = jnp.where(hm, q * ATTN_SCALE, drop).astype(BF16)
            bias_of = lambda j, i=i: causal_ref[...] if j == i else None
            outs.append(_attend(qh, _blocks_of(kaug_ref), v_of, bias_of, i + 1, s_refs.at[h], p_refs.at[h]))
        o_ref[0, i * t:(i + 1) * t, :] = jnp.where(lo, outs[0], outs[1]).astype(BF16)


def _moba_attention(qkv, q_tile0, k_tile0, v_tile0, n_pairs):
    b, s, _ = qkv.shape
    t = MOBA_BLOCK
    assert t == ATT_TILE
    col = lambda c0: pl.BlockSpec((1, s, LANES), lambda b_, p: (b_, 0, c0 + p))
    return pl.pallas_call(
        _moba_kernel,
        grid=(b, n_pairs),
        in_specs=[col(q_tile0), col(k_tile0), col(v_tile0), pl.BlockSpec((t, t), lambda b_, p: (0, 0))],
        out_specs=col(0),
        out_shape=jax.ShapeDtypeStruct((b, s, n_pairs * LANES), BF16),
        scratch_shapes=[pltpu.VMEM((s, LANES), BF16), pltpu.VMEM((s, LANES), BF16),
                        pltpu.VMEM((LANES, LANES), F32)] + _attn_scratch(s),
        compiler_params=_cparams(("parallel", "parallel")),
        name="moba_attention",
    )(qkv, qkv, qkv, _causal_table(t))


def _diff_kernel(q_ref, k_ref, v_ref, causal_ref, lq1_ref, lk1_ref, lq2_ref, lk2_ref, g_ref, o_ref,
                 s_refs, p_refs, *, lambda_init):
    t = ATT_TILE
    k_of, v_of = _blocks_of(k_ref), _blocks_of(v_ref)
    lam = (jnp.exp(jnp.sum(lq1_ref[...] * lk1_ref[...], axis=-1, keepdims=True))
           - jnp.exp(jnp.sum(lq2_ref[...] * lk2_ref[...], axis=-1, keepdims=True)) + lambda_init)
    for i in range(q_ref.shape[1] // t):
        q = q_ref[0, i * t:(i + 1) * t, :].astype(F32)
        outs = []
        for h, hm in enumerate(_head_masks(q.shape)):
            qh = jnp.where(hm, q * ATTN_SCALE, 0.0).astype(BF16)
            bias_of = lambda j, i=i: causal_ref[...] if j == i else None
            outs.append(_attend(qh, k_of, v_of, bias_of, i + 1, s_refs.at[h], p_refs.at[h]))
        o = outs[0] - lam * outs[1]
        y = o * lax.rsqrt(jnp.mean(o * o, axis=-1, keepdims=True) + NORM_EPS)
        o_ref[0, i * t:(i + 1) * t, :] = ((y * g_ref[...]) * (1.0 - lambda_init)).astype(BF16)


def _diff_attention(qkv, lq1, lk1, lq2, lk2, subln_g, lambda_init, n_heads):
    b, s, _ = qkv.shape
    t = ATT_TILE
    vec = lambda a: a.reshape(1, -1).astype(F32)
    small = lambda n: pl.BlockSpec((1, n), lambda b_, h: (0, 0))
    col = lambda c0: pl.BlockSpec((1, s, LANES), lambda b_, h: (b_, 0, c0 + h))
    return pl.pallas_call(
        functools.partial(_diff_kernel, lambda_init=lambda_init),
        grid=(b, n_heads),
        in_specs=[col(0), col(n_heads), col(2 * n_heads), pl.BlockSpec((t, t), lambda b_, h: (0, 0)),
                  small(HEAD_DIM), small(HEAD_DIM), small(HEAD_DIM), small(HEAD_DIM), small(LANES)],
        out_specs=col(0),
        out_shape=jax.ShapeDtypeStruct((b, s, n_heads * LANES), BF16),
        scratch_shapes=_attn_scratch(s),
        compiler_params=_cparams(("parallel", "parallel")),
        name="diff_attention",
    )(qkv, qkv, qkv, _causal_table(t), vec(lq1), vec(lk1), vec(lq2), vec(lk2), vec(subln_g))


def _out_proj_kernel(*refs):
    h_ref, o_ref = refs[0], refs[-1]
    acts_ws = refs[1:-1]
    acc = h_ref[...]
    for a_ref, w_ref in zip(acts_ws[0::2], acts_ws[1::2]):
        acc = acc + jnp.dot(a_ref[...], w_ref[...], preferred_element_type=F32)
    o_ref[...] = acc


def _out_proj(h, acts, ws):
    t, d = h.shape
    tm = ROW_TILE
    row = lambda i: (i, 0)
    fixed = lambda i: (0, 0)
    in_specs = [pl.BlockSpec((tm, d), row)]
    args = [h]
    for a, w in zip(acts, ws):
        in_specs += [pl.BlockSpec((tm, a.shape[1]), row), pl.BlockSpec(w.shape, fixed)]
        args += [a, w]
    return pl.pallas_call(
        _out_proj_kernel,
        grid=(t // tm,),
        in_specs=in_specs,
        out_specs=pl.BlockSpec((tm, d), row),
        out_shape=jax.ShapeDtypeStruct((t, d), F32),
        compiler_params=_cparams(("parallel",)),
        name="out_proj_residual",
    )(*args)


def _ffn_kernel(h_ref, g_ref, wg_ref, wu_ref, wd_ref, gf_ref, o_ref, *, chunk, final_norm):
    x = h_ref[...]
    ms = jnp.mean(x * x, axis=-1, keepdims=True)
    xn = (x * lax.rsqrt(ms + NORM_EPS) * g_ref[...]).astype(BF16)
    acc = x
    for c in range(wg_ref.shape[1] // chunk):
        sl = slice(c * chunk, (c + 1) * chunk)
        gate = jnp.dot(xn, wg_ref[:, sl], preferred_element_type=F32)
        up = jnp.dot(xn, wu_ref[:, sl], preferred_element_type=F32)
        mid = (gate * jax.nn.sigmoid(gate) * up).astype(BF16)
        acc = acc + jnp.dot(mid, wd_ref[sl, :], preferred_element_type=F32)
    if final_norm:
        ms = jnp.mean(acc * acc, axis=-1, keepdims=True)
        acc = acc * lax.rsqrt(ms + NORM_EPS) * gf_ref[...]
    o_ref[...] = acc


def _ffn(h, g, wg, wu, wd, g_final, final_norm):
    t, d = h.shape
    ff = wg.shape[1]
    tm = ROW_TILE
    row = lambda i: (i, 0)
    fixed = lambda i: (0, 0)
    once = dict(pipeline_mode=pl.Buffered(1))
    return pl.pallas_call(
        functools.partial(_ffn_kernel, chunk=256, final_norm=final_norm),
        grid=(t // tm,),
        in_specs=[pl.BlockSpec((tm, d), row), pl.BlockSpec((1, d), fixed),
                  pl.BlockSpec((d, ff), fixed, **once), pl.BlockSpec((d, ff), fixed, **once),
                  pl.BlockSpec((ff, d), fixed, **once), pl.BlockSpec((1, d), fixed)],
        out_specs=pl.BlockSpec((tm, d), row),
        out_shape=jax.ShapeDtypeStruct((t, d), F32),
        compiler_params=_cparams(("parallel",)),
        name="ffn_swiglu",
    )(h, g.reshape(1, d), wg, wu, wd, g_final.reshape(1, d))


def _lambda_init(layer_idx):
    return 0.8 - 0.6 * math.exp(-0.3 * layer_idx)


def kernel(x, positions, ab_norm_g, ab_w_in, ab_w_out, diff_norm_g, diff_w_in, diff_w_out, diff_lambda_q1, diff_lambda_k1, diff_lambda_q2, diff_lambda_k2, diff_subln_g, ffn_norm_g, ffn_w_gate, ffn_w_up, ffn_w_down, final_norm_g):
    b, s, d = x.shape
    t = b * s
    n_pairs = d // (4 * HEAD_DIM)
    n_diff = d // (2 * HEAD_DIM)
    tabs = _rope_tables(positions)
    h = x.reshape(t, d)

    rope0 = [c for c in range(6 * n_pairs) if (c // n_pairs) in (0, 1, 3, 4)]
    qkv = _norm_proj(h, ab_norm_g[0], ab_w_in[0].astype(BF16), tabs, rope0).reshape(b, s, -1)
    oa = _dilated_attention(qkv, 0, n_pairs, 2 * n_pairs, n_pairs)
    ob = _moba_attention(qkv, 3 * n_pairs, 4 * n_pairs, 5 * n_pairs, n_pairs)
    w_out = ab_w_out[0].astype(BF16)
    wa = n_pairs * LANES
    h = _out_proj(h, [oa.reshape(t, wa), ob.reshape(t, wa)], [w_out[:wa], w_out[wa:]])
    h = _ffn(h, ffn_norm_g[0], ffn_w_gate[0].astype(BF16), ffn_w_up[0].astype(BF16),
             ffn_w_down[0].astype(BF16), final_norm_g, False)

    rope1 = list(range(2 * n_diff))
    qkv = _norm_proj(h, diff_norm_g[0], diff_w_in[0].astype(BF16), tabs, rope1).reshape(b, s, -1)
    od = _diff_attention(qkv, diff_lambda_q1[0], diff_lambda_k1[0], diff_lambda_q2[0], diff_lambda_k2[0],
                         diff_subln_g[0], _lambda_init(1), n_diff)
    h = _out_proj(h, [od.reshape(t, d)], [diff_w_out[0].astype(BF16)])
    h = _ffn(h, ffn_norm_g[1], ffn_w_gate[1].astype(BF16), ffn_w_up[1].astype(BF16),
             ffn_w_down[1].astype(BF16), final_norm_g, True)
    return h.reshape(b, s, d)
```

```python
import functools
import math

import numpy as np
import jax
import jax.numpy as jnp
from jax import lax
from jax.experimental import pallas as pl
from jax.experimental.pallas import tpu as pltpu

D_MODEL = 1024
HEAD_DIM = 64
LANES = 128
SUBLANES = 8
ROPE_DIM = HEAD_DIM // 4
ROPE_HALF = ROPE_DIM // 2
ROPE_THETA = 500000.0
D_FF = 2816
NORM_EPS = 1e-5
ATTN_SCALE = HEAD_DIM ** -0.5
NEG = -1e30
DILATED_CONFIGS = ((128, 1), (512, 4), (2048, 16))
MOBA_BLOCK = 256
MOBA_TOPK = 3

ATT_TILE = 256
ROW_TILE = 512
VMEM_LIMIT = 56 * 1024 * 1024

F32 = jnp.float32
BF16 = jnp.bfloat16
_NT = (((1,), (1,)), ((), ()))

assert ROPE_HALF == SUBLANES


def _cparams(sem):
    return pltpu.CompilerParams(dimension_semantics=sem, vmem_limit_bytes=VMEM_LIMIT)


def _rope_table_kernel(pos_col_ref, pos_row_ref, invf_lane_ref, invf_col_ref,
                       c_ref, s1_ref, s2_ref, ct_ref, st_ref):
    ang = pos_col_ref[...].astype(F32) * invf_lane_ref[...]
    lane = lax.broadcasted_iota(jnp.int32, ang.shape, 1) % HEAD_DIM
    cos, sin = jnp.cos(ang), jnp.sin(ang)
    c_ref[...] = jnp.where(lane < ROPE_DIM, cos, 1.0)
    s1_ref[...] = jnp.where(lane < ROPE_HALF, -sin, 0.0)
    s2_ref[...] = jnp.where((lane >= ROPE_HALF) & (lane < ROPE_DIM), sin, 0.0)
    ang_t = invf_col_ref[...] * pos_row_ref[...].astype(F32)
    ct_ref[...] = jnp.cos(ang_t)
    st_ref[...] = jnp.sin(ang_t)


def _rope_tables(positions):
    t = positions.size
    tm = 1024
    inv_freq = ROPE_THETA ** (-jnp.arange(0, ROPE_DIM, 2, dtype=F32) / ROPE_DIM)
    lane = np.arange(LANES) % HEAD_DIM
    invf_lane = jnp.where(lane < ROPE_DIM, inv_freq[lane % ROPE_HALF], 0.0).reshape(1, LANES)
    tab = jax.ShapeDtypeStruct((t, LANES), F32)
    tab_t = jax.ShapeDtypeStruct((ROPE_HALF, t), F32)
    row = lambda i: (i, 0)
    col = lambda i: (0, i)
    fixed = lambda i: (0, 0)
    return pl.pallas_call(
        _rope_table_kernel,
        grid=(t // tm,),
        in_specs=[pl.BlockSpec((tm, 1), row), pl.BlockSpec((1, tm), col),
                  pl.BlockSpec((1, LANES), fixed), pl.BlockSpec((ROPE_HALF, 1), fixed)],
        out_specs=[pl.BlockSpec((tm, LANES), row)] * 3 + [pl.BlockSpec((ROPE_HALF, tm), col)] * 2,
        out_shape=[tab] * 3 + [tab_t] * 2,
        compiler_params=_cparams(("parallel",)),
        name="rope_tables",
    )(positions.reshape(t, 1), positions.reshape(1, t), invf_lane, inv_freq.reshape(ROPE_HALF, 1))


def _norm_proj_kernel(x_ref, g_ref, wk_ref, wt_ref, c_ref, s1_ref, s2_ref, ct_ref, st_ref,
                      k_ref, qvt_ref, acc_ref, *, q_tiles):
    x = x_ref[...]
    ms = jnp.mean(x * x, axis=-1, keepdims=True)
    xn = (x * lax.rsqrt(ms + NORM_EPS) * g_ref[...]).astype(BF16)

    c, s1, s2 = c_ref[...], s1_ref[...], s2_ref[...]
    chunk = 512
    for n in range(k_ref.shape[1] // chunk):
        acc = jnp.dot(xn, wk_ref[:, n * chunk:(n + 1) * chunk], preferred_element_type=F32)
        for t in range(chunk // LANES):
            seg = acc[:, t * LANES:(t + 1) * LANES]
            seg = seg * c + pltpu.roll(seg, LANES - ROPE_HALF, 1) * s1 + pltpu.roll(seg, ROPE_HALF, 1) * s2
            col = n * chunk + t * LANES
            k_ref[:, col:col + LANES] = seg.astype(BF16)

    acc_ref[...] = lax.dot_general(wt_ref[...], xn, _NT, preferred_element_type=F32)
    ct, st = ct_ref[...], st_ref[...]
    for r in range(qvt_ref.shape[0] // LANES):
        if r not in q_tiles:
            qvt_ref[r * LANES:(r + 1) * LANES, :] = acc_ref[r * LANES:(r + 1) * LANES, :].astype(BF16)
            continue
        for r0 in range(r * LANES, (r + 1) * LANES, HEAD_DIM):
            x1 = acc_ref[r0:r0 + ROPE_HALF, :]
            x2 = acc_ref[r0 + ROPE_HALF:r0 + ROPE_DIM, :]
            rot = jnp.concatenate([x1 * ct - x2 * st, x2 * ct + x1 * st], axis=0)
            qvt_ref[r0:r0 + ROPE_DIM, :] = rot.astype(BF16)
            qvt_ref[r0 + ROPE_DIM:r0 + HEAD_DIM, :] = acc_ref[r0 + ROPE_DIM:r0 + HEAD_DIM, :].astype(BF16)


def _norm_proj(x, g, wk, wt, tabs, q_tiles):
    t, d = x.shape
    nk, nt = wk.shape[1], wt.shape[0]
    tm = ROW_TILE
    row = lambda i: (i, 0)
    col = lambda i: (0, i)
    fixed = lambda i: (0, 0)
    return pl.pallas_call(
        functools.partial(_norm_proj_kernel, q_tiles=frozenset(q_tiles)),
        grid=(t // tm,),
        in_specs=[pl.BlockSpec((tm, d), row), pl.BlockSpec((1, d), fixed),
                  pl.BlockSpec((d, nk), fixed), pl.BlockSpec((nt, d), fixed),
                  pl.BlockSpec((tm, LANES), row), pl.BlockSpec((tm, LANES), row), pl.BlockSpec((tm, LANES), row),
                  pl.BlockSpec((ROPE_HALF, tm), col), pl.BlockSpec((ROPE_HALF, tm), col)],
        out_specs=[pl.BlockSpec((tm, nk), row), pl.BlockSpec((nt, tm), col)],
        out_shape=[jax.ShapeDtypeStruct((t, nk), BF16), jax.ShapeDtypeStruct((nt, t), BF16)],
        scratch_shapes=[pltpu.VMEM((nt, tm), F32)],
        compiler_params=_cparams(("parallel",)),
        name="norm_proj_rope",
    )(x, g.reshape(1, d), wk, wt, *tabs)


def _row_masks(shape):
    row = lax.broadcasted_iota(jnp.int32, shape, 0)
    return row < HEAD_DIM, row >= HEAD_DIM


def _causal_table_t(t):
    d = np.arange(t)[None, :] - np.arange(t)[:, None]
    return jnp.asarray(np.where(d >= 0, 0.0, NEG), F32)


def _logits_stage(qt, k_of, bias_of, n_blocks, s_ref):
    t = ATT_TILE
    m = None
    for j in range(n_blocks):
        s = jnp.dot(k_of(j), qt, preferred_element_type=F32)
        bias = bias_of(j)
        if bias is not None:
            s = s + bias
        s_ref[j * t:(j + 1) * t, :] = s
        mj = jnp.max(s, axis=0, keepdims=True)
        m = mj if m is None else jnp.maximum(m, mj)
    return m


def _probs_stage(m, n_blocks, s_ref, p_ref):
    t = ATT_TILE
    l = None
    for j in range(n_blocks):
        p = jnp.exp(s_ref[j * t:(j + 1) * t, :] - m)
        p_ref[j * t:(j + 1) * t, :] = p.astype(BF16)
        lj = jnp.sum(p, axis=0, keepdims=True)
        l = lj if l is None else l + lj
    return l


def _values_stage(l, vt_of, n_blocks, p_ref):
    n = n_blocks * ATT_TILE
    acc = jnp.dot(vt_of(n), p_ref[:n, :], preferred_element_type=F32)
    return acc * (1.0 / l)


def _pipelined_attention(n_tiles, make_query, k_of_head, bias_of_tile, vt_of, finish_tile, s_refs, p_refs):
    items = [(i, h) for i in range(n_tiles) for h in range(2)]
    n = len(items)
    ms, ls, outs = {}, {}, {}
    for step in range(n + 2):
        if step < n:
            i, h = items[step]
            ms[step] = _logits_stage(make_query(i, h), k_of_head(h), bias_of_tile(i), i + 1, s_refs.at[step % 2])
        if 1 <= step <= n:
            c = step - 1
            ls[c] = _probs_stage(ms.pop(c), items[c][0] + 1, s_refs.at[c % 2], p_refs.at[c % 2])
        if 2 <= step <= n + 1:
            c = step - 2
            i, h = items[c]
            outs[c] = _values_stage(ls.pop(c), vt_of, i + 1, p_refs.at[c % 2])
            if h == 1:
                finish_tile(i, outs.pop(c - 1), outs.pop(c))


def _key_blocks(ref):
    t = ATT_TILE
    if len(ref.shape) == 2:
        return lambda j: ref[j * t:(j + 1) * t, :]
    return lambda j: ref[0, j * t:(j + 1) * t, :]


def _attn_scratch(s):
    return [pltpu.VMEM((2, s, ATT_TILE), F32), pltpu.VMEM((2, s, ATT_TILE), BF16)]


def _k_spec(s, tile0):
    return pl.BlockSpec((1, s, LANES), lambda b, p: (b, 0, tile0 + p))


def _t_spec(s, tile0):
    return pl.BlockSpec((LANES, s), lambda b, p: (tile0 + p, b))


def _dilated_bias_tables_t(t):
    windows = sorted(w for w, _ in DILATED_CONFIGS)
    far = windows[-2] // t + 1
    tabs = []
    for delta in range(far + 1):
        d = delta * t + np.arange(t)[None, :] - np.arange(t)[:, None]
        mult = np.zeros((t, t), np.float64)
        for window, dil in DILATED_CONFIGS:
            mult += (d >= 0) & (d % dil == 0) & (d <= window)
        tabs.append(np.where(mult > 0, np.log(np.maximum(mult, 1.0)), NEG))
    return jnp.asarray(np.stack(tabs), F32)


def _dilated_kernel(qt_ref, k_ref, vt_ref, bias_ref, o_ref, s_refs, p_refs):
    t = ATT_TILE
    n_tab = bias_ref.shape[0]
    k_of = _key_blocks(k_ref)

    def make_query(i, h):
        qt = qt_ref[:, i * t:(i + 1) * t].astype(F32)
        return jnp.where(_row_masks(qt.shape)[h], qt, 0.0).astype(BF16)

    def finish_tile(i, out_lo, out_hi):
        lo, _ = _row_masks(out_lo.shape)
        o_ref[0, i * t:(i + 1) * t, :] = jnp.where(lo, out_lo, out_hi).T.astype(BF16)

    _pipelined_attention(qt_ref.shape[1] // t, make_query, lambda h: k_of,
                         lambda i: (lambda j: bias_ref[min(i - j, n_tab - 1)]),
                         lambda n: vt_ref[:, :n], finish_tile, s_refs, p_refs)


def _dilated_attention(kn, qvt, b, s, q_tile0, k_tile0, v_tile0, n_pairs):
    bias = _dilated_bias_tables_t(ATT_TILE)
    return pl.pallas_call(
        _dilated_kernel,
        grid=(b, n_pairs),
        in_specs=[_t_spec(s, q_tile0), _k_spec(s, k_tile0), _t_spec(s, v_tile0),
                  pl.BlockSpec(bias.shape, lambda b_, p: (0, 0, 0))],
        out_specs=_k_spec(s, 0),
        out_shape=jax.ShapeDtypeStruct((b, s, n_pairs * LANES), BF16),
        scratch_shapes=_attn_scratch(s),
        compiler_params=_cparams(("parallel", "parallel")),
        name="dilated_attention",
    )(qvt, kn, qvt, bias)


def _moba_kernel(qt_ref, k_ref, vt_ref, causal_ref, o_ref, ka_ref, kb_ref, s_refs, p_refs):
    t = MOBA_BLOCK
    s_len = k_ref.shape[1]
    nblk = s_len // t

    k = k_ref[0].astype(F32)
    lane_k = lax.broadcasted_iota(jnp.int32, k.shape, 1)
    blk = lax.broadcasted_iota(jnp.int32, k.shape, 0) // t
    ka_ref[...] = jnp.where(lane_k < HEAD_DIM, k, jnp.where(lane_k - HEAD_DIM == blk, 1.0, 0.0)).astype(BF16)
    kb_ref[...] = jnp.where(lane_k >= HEAD_DIM, k, jnp.where(lane_k == blk, 1.0, 0.0)).astype(BF16)
    means = jnp.concatenate(
        [jnp.mean(k_ref[0, n * t:(n + 1) * t, :].astype(F32), axis=0, keepdims=True) for n in range(nblk)], axis=0)
    lane_m = lax.broadcasted_iota(jnp.int32, means.shape, 1)
    means2 = jnp.concatenate([jnp.where(lane_m < HEAD_DIM, means, 0.0),
                              jnp.where(lane_m >= HEAD_DIM, means, 0.0)], axis=0)

    zeros = jnp.zeros((HEAD_DIM - nblk, t), F32)
    k_of = (_key_blocks(ka_ref), _key_blocks(kb_ref))

    def make_query(i, h):
        qt = qt_ref[:, i * t:(i + 1) * t].astype(F32)
        if i <= MOBA_TOPK:
            return jnp.where(_row_masks(qt.shape)[h], qt, 0.0).astype(BF16)
        g = jnp.dot(means2[h * nblk:(h + 1) * nblk], qt, preferred_element_type=F32,
                    precision=lax.Precision.HIGHEST)
        row = lax.broadcasted_iota(jnp.int32, g.shape, 0)
        rank = jnp.zeros(g.shape, F32)
        for kk in range(i):
            gk = g[kk:kk + 1, :]
            beats = (gk > g) | ((gk == g) & (row > kk))
            rank = rank + jnp.where(beats, 1.0, 0.0)
        drop = jnp.where((row < i) & (rank >= MOBA_TOPK), NEG, 0.0)
        parts = [qt[:HEAD_DIM], drop, zeros] if h == 0 else [drop, zeros, qt[HEAD_DIM:]]
        return jnp.concatenate(parts, axis=0).astype(BF16)

    def finish_tile(i, out_lo, out_hi):
        lo, _ = _row_masks(out_lo.shape)
        o_ref[0, i * t:(i + 1) * t, :] = jnp.where(lo, out_lo, out_hi).T.astype(BF16)

    _pipelined_attention(nblk, make_query, lambda h: k_of[h],
                         lambda i: (lambda j: causal_ref[...] if j == i else None),
                         lambda n: vt_ref[:, :n], finish_tile, s_refs, p_refs)


def _moba_attention(kn, qvt, b, s, q_tile0, k_tile0, v_tile0, n_pairs):
    t = MOBA_BLOCK
    assert t == ATT_TILE and s // t <= SUBLANES
    return pl.pallas_call(
        _moba_kernel,
        grid=(b, n_pairs),
        in_specs=[_t_spec(s, q_tile0), _k_spec(s, k_tile0), _t_spec(s, v_tile0),
                  pl.BlockSpec((t, t), lambda b_, p: (0, 0))],
        out_specs=_k_spec(s, 0),
        out_shape=jax.ShapeDtypeStruct((b, s, n_pairs * LANES), BF16),
        scratch_shapes=[pltpu.VMEM((s, LANES), BF16), pltpu.VMEM((s, LANES), BF16)] + _attn_scratch(s),
        compiler_params=_cparams(("parallel", "parallel")),
        name="moba_attention",
    )(qvt, kn, qvt, _causal_table_t(t))


def _diff_kernel(qt_ref, k_ref, vt_ref, causal_ref, lq1_ref, lk1_ref, lq2_ref, lk2_ref, g_ref, o_ref,
                 s_refs, p_refs, *, lambda_init):
    t = ATT_TILE
    k_of = _key_blocks(k_ref)
    lam = (jnp.exp(jnp.sum(lq1_ref[...] * lk1_ref[...], axis=-1, keepdims=True))
           - jnp.exp(jnp.sum(lq2_ref[...] * lk2_ref[...], axis=-1, keepdims=True)) + lambda_init)

    def make_query(i, h):
        qt = qt_ref[:, i * t:(i + 1) * t].astype(F32)
        return jnp.where(_row_masks(qt.shape)[h], qt, 0.0).astype(BF16)

    def finish_tile(i, out1, out2):
        o = out1 - lam * out2
        y = o * lax.rsqrt(jnp.mean(o * o, axis=0, keepdims=True) + NORM_EPS)
        o_ref[0, i * t:(i + 1) * t, :] = ((y.T * g_ref[...]) * (1.0 - lambda_init)).astype(BF16)

    _pipelined_attention(qt_ref.shape[1] // t, make_query, lambda h: k_of,
                         lambda i: (lambda j: causal_ref[...] if j == i else None),
                         lambda n: vt_ref[:, :n], finish_tile, s_refs, p_refs)


def _diff_attention(kn, qvt, b, s, lq1, lk1, lq2, lk2, subln_g, lambda_init, n_heads):
    t = ATT_TILE
    vec = lambda a: a.reshape(1, -1).astype(F32)
    small = lambda n: pl.BlockSpec((1, n), lambda b_, h: (0, 0))
    return pl.pallas_call(
        functools.partial(_diff_kernel, lambda_init=lambda_init),
        grid=(b, n_heads),
        in_specs=[_t_spec(s, 0), _k_spec(s, 0), _t_spec(s, n_heads), pl.BlockSpec((t, t), lambda b_, h: (0, 0)),
                  small(HEAD_DIM), small(HEAD_DIM), small(HEAD_DIM), small(HEAD_DIM), small(LANES)],
        out_specs=_k_spec(s, 0),
        out_shape=jax.ShapeDtypeStruct((b, s, n_heads * LANES), BF16),
        scratch_shapes=_attn_scratch(s),
        compiler_params=_cparams(("parallel", "parallel")),
        name="diff_attention",
    )(qvt, kn, qvt, _causal_table_t(t), vec(lq1), vec(lk1), vec(lq2), vec(lk2), vec(subln_g))


def _out_proj_kernel(*refs):
    h_ref, o_ref = refs[0], refs[-1]
    acts_ws = refs[1:-1]
    acc = h_ref[...]
    for a_ref, w_ref in zip(acts_ws[0::2], acts_ws[1::2]):
        acc = acc + jnp.dot(a_ref[...], w_ref[...], preferred_element_type=F32)
    o_ref[...] = acc


def _out_proj(h, acts, ws):
    t, d = h.shape
    tm = ROW_TILE
    row = lambda i: (i, 0)
    fixed = lambda i: (0, 0)
    in_specs = [pl.BlockSpec((tm, d), row)]
    args = [h]
    for a, w in zip(acts, ws):
        in_specs += [pl.BlockSpec((tm, a.shape[1]), row), pl.BlockSpec(w.shape, fixed)]
        args += [a, w]
    return pl.pallas_call(
        _out_proj_kernel,
        grid=(t // tm,),
        in_specs=in_specs,
        out_specs=pl.BlockSpec((tm, d), row),
        out_shape=jax.ShapeDtypeStruct((t, d), F32),
        compiler_params=_cparams(("parallel",)),
        name="out_proj_residual",
    )(*args)


def _ffn_kernel(h_ref, g_ref, wg_ref, wu_ref, wd_ref, gf_ref, o_ref, *, chunk, final_norm):
    x = h_ref[...]
    ms = jnp.mean(x * x, axis=-1, keepdims=True)
    xn = (x * lax.rsqrt(ms + NORM_EPS) * g_ref[...]).astype(BF16)
    acc = x
    for c in range(wg_ref.shape[1] // chunk):
        sl = slice(c * chunk, (c + 1) * chunk)
        gate = jnp.dot(xn, wg_ref[:, sl], preferred_element_type=F32)
        up = jnp.dot(xn, wu_ref[:, sl], preferred_element_type=F32)
        mid = (gate * jax.nn.sigmoid(gate) * up).astype(BF16)
        acc = acc + jnp.dot(mid, wd_ref[sl, :], preferred_element_type=F32)
    if final_norm:
        ms = jnp.mean(acc * acc, axis=-1, keepdims=True)
        acc = acc * lax.rsqrt(ms + NORM_EPS) * gf_ref[...]
    o_ref[...] = acc


def _ffn(h, g, wg, wu, wd, g_final, final_norm):
    t, d = h.shape
    ff = wg.shape[1]
    tm = ROW_TILE
    row = lambda i: (i, 0)
    fixed = lambda i: (0, 0)
    once = dict(pipeline_mode=pl.Buffered(1))
    return pl.pallas_call(
        functools.partial(_ffn_kernel, chunk=256, final_norm=final_norm),
        grid=(t // tm,),
        in_specs=[pl.BlockSpec((tm, d), row), pl.BlockSpec((1, d), fixed),
                  pl.BlockSpec((d, ff), fixed, **once), pl.BlockSpec((d, ff), fixed, **once),
                  pl.BlockSpec((ff, d), fixed, **once), pl.BlockSpec((1, d), fixed)],
        out_specs=pl.BlockSpec((tm, d), row),
        out_shape=jax.ShapeDtypeStruct((t, d), F32),
        compiler_params=_cparams(("parallel",)),
        name="ffn_swiglu",
    )(h, g.reshape(1, d), wg, wu, wd, g_final.reshape(1, d))


def _lambda_init(layer_idx):
    return 0.8 - 0.6 * math.exp(-0.3 * layer_idx)


def _split_weights(w_in, q_cols, k_cols, v_cols):
    cat = lambda cols: jnp.concatenate([w_in[:, a:b] for a, b in cols], axis=1)
    groups = [(cat([c]) * ATTN_SCALE if is_q else cat([c])) for c, is_q in
              sorted([(c, True) for c in q_cols] + [(c, False) for c in v_cols])]
    return cat(k_cols).astype(BF16), jnp.concatenate(groups, axis=1).T.astype(BF16)


def kernel(x, positions, ab_norm_g, ab_w_in, ab_w_out, diff_norm_g, diff_w_in, diff_w_out, diff_lambda_q1, diff_lambda_k1, diff_lambda_q2, diff_lambda_k2, diff_subln_g, ffn_norm_g, ffn_w_gate, ffn_w_up, ffn_w_down, final_norm_g):
    b, s, d = x.shape
    t = b * s
    n_pairs = d // (4 * HEAD_DIM)
    n_diff = d // (2 * HEAD_DIM)
    wa = n_pairs * LANES
    tabs = _rope_tables(positions)
    h = x.reshape(t, d)

    wk, wt = _split_weights(ab_w_in[0], q_cols=[(0, wa), (3 * wa, 4 * wa)],
                            k_cols=[(wa, 2 * wa), (4 * wa, 5 * wa)], v_cols=[(2 * wa, 3 * wa), (5 * wa, 6 * wa)])
    q_tiles0 = list(range(n_pairs)) + list(range(2 * n_pairs, 3 * n_pairs))
    kn, qvt = _norm_proj(h, ab_norm_g[0], wk, wt, tabs, q_tiles0)
    kn = kn.reshape(b, s, -1)
    oa = _dilated_attention(kn, qvt, b, s, 0, 0, n_pairs, n_pairs)
    ob = _moba_attention(kn, qvt, b, s, 2 * n_pairs, n_pairs, 3 * n_pairs, n_pairs)
    w_out = ab_w_out[0].astype(BF16)
    h = _out_proj(h, [oa.reshape(t, wa), ob.reshape(t, wa)], [w_out[:wa], w_out[wa:]])
    h = _ffn(h, ffn_norm_g[0], ffn_w_gate[0].astype(BF16), ffn_w_up[0].astype(BF16),
             ffn_w_down[0].astype(BF16), final_norm_g, False)

    wk, wt = _split_weights(diff_w_in[0], q_cols=[(0, d)], k_cols=[(d, 2 * d)], v_cols=[(2 * d, 3 * d)])
    kn, qvt = _norm_proj(h, diff_norm_g[0], wk, wt, tabs, list(range(n_diff)))
    od = _diff_attention(kn.reshape(b, s, -1), qvt, b, s, diff_lambda_q1[0], diff_lambda_k1[0],
                         diff_lambda_q2[0], diff_lambda_k2[0], diff_subln_g[0], _lambda_init(1), n_diff)
    h = _out_proj(h, [od.reshape(t, d)], [diff_w_out[0].astype(BF16)])
    h = _ffn(h, ffn_norm_g[1], ffn_w_gate[1].astype(BF16), ffn_w_up[1].astype(BF16),
             ffn_w_down[1].astype(BF16), final_norm_g, True)
    return h.reshape(b, s, d)
```

```python
import functools
import math

import numpy as np
import jax
import jax.numpy as jnp
from jax import lax
from jax.experimental import pallas as pl
from jax.experimental.pallas import tpu as pltpu

D_MODEL = 1024
HEAD_DIM = 64
LANES = 128
SUBLANES = 8
ROPE_DIM = HEAD_DIM // 4
ROPE_HALF = ROPE_DIM // 2
ROPE_THETA = 500000.0
D_FF = 2816
NORM_EPS = 1e-5
ATTN_SCALE = HEAD_DIM ** -0.5
LOG2E = math.log2(math.e)
NEG = -1e30
DILATED_CONFIGS = ((128, 1), (512, 4), (2048, 16))
MOBA_BLOCK = 256
MOBA_TOPK = 3

ATT_TILE = 256
ROW_TILE = 512
VMEM_LIMIT = 56 * 1024 * 1024

F32 = jnp.float32
BF16 = jnp.bfloat16
_NT = (((1,), (1,)), ((), ()))

assert ROPE_HALF == SUBLANES


def _cparams(sem):
    return pltpu.CompilerParams(dimension_semantics=sem, vmem_limit_bytes=VMEM_LIMIT)


def _rope_table_kernel(pos_col_ref, pos_row_ref, invf_lane_ref, invf_col_ref,
                       c_ref, s1_ref, s2_ref, ct_ref, st_ref):
    ang = pos_col_ref[...].astype(F32) * invf_lane_ref[...]
    lane = lax.broadcasted_iota(jnp.int32, ang.shape, 1) % HEAD_DIM
    cos, sin = jnp.cos(ang), jnp.sin(ang)
    c_ref[...] = jnp.where(lane < ROPE_DIM, cos, 1.0)
    s1_ref[...] = jnp.where(lane < ROPE_HALF, -sin, 0.0)
    s2_ref[...] = jnp.where((lane >= ROPE_HALF) & (lane < ROPE_DIM), sin, 0.0)
    ang_t = invf_col_ref[...] * pos_row_ref[...].astype(F32)
    ct_ref[...] = jnp.cos(ang_t)
    st_ref[...] = jnp.sin(ang_t)


def _rope_tables(positions):
    t = positions.size
    tm = 1024
    inv_freq = ROPE_THETA ** (-jnp.arange(0, ROPE_DIM, 2, dtype=F32) / ROPE_DIM)
    lane = np.arange(LANES) % HEAD_DIM
    invf_lane = jnp.where(lane < ROPE_DIM, inv_freq[lane % ROPE_HALF], 0.0).reshape(1, LANES)
    tab = jax.ShapeDtypeStruct((t, LANES), F32)
    tab_t = jax.ShapeDtypeStruct((ROPE_HALF, t), F32)
    row = lambda i: (i, 0)
    col = lambda i: (0, i)
    fixed = lambda i: (0, 0)
    return pl.pallas_call(
        _rope_table_kernel,
        grid=(t // tm,),
        in_specs=[pl.BlockSpec((tm, 1), row), pl.BlockSpec((1, tm), col),
                  pl.BlockSpec((1, LANES), fixed), pl.BlockSpec((ROPE_HALF, 1), fixed)],
        out_specs=[pl.BlockSpec((tm, LANES), row)] * 3 + [pl.BlockSpec((ROPE_HALF, tm), col)] * 2,
        out_shape=[tab] * 3 + [tab_t] * 2,
        compiler_params=_cparams(("parallel",)),
        name="rope_tables",
    )(positions.reshape(t, 1), positions.reshape(1, t), invf_lane, inv_freq.reshape(ROPE_HALF, 1))


def _norm_proj_kernel(x_ref, g_ref, wk_ref, wt_ref, c_ref, s1_ref, s2_ref, ct_ref, st_ref,
                      k_ref, qvt_ref, acc_ref, *, q_tiles):
    x = x_ref[...]
    ms = jnp.mean(x * x, axis=-1, keepdims=True)
    xn = (x * lax.rsqrt(ms + NORM_EPS) * g_ref[...]).astype(BF16)

    c, s1, s2 = c_ref[...], s1_ref[...], s2_ref[...]
    chunk = 512
    for n in range(k_ref.shape[1] // chunk):
        acc = jnp.dot(xn, wk_ref[:, n * chunk:(n + 1) * chunk], preferred_element_type=F32)
        for t in range(chunk // LANES):
            seg = acc[:, t * LANES:(t + 1) * LANES]
            seg = seg * c + pltpu.roll(seg, LANES - ROPE_HALF, 1) * s1 + pltpu.roll(seg, ROPE_HALF, 1) * s2
            col = n * chunk + t * LANES
            k_ref[:, col:col + LANES] = seg.astype(BF16)

    acc_ref[...] = lax.dot_general(wt_ref[...], xn, _NT, preferred_element_type=F32)
    ct, st = ct_ref[...], st_ref[...]
    for r in range(qvt_ref.shape[0] // LANES):
        if r not in q_tiles:
            qvt_ref[r * LANES:(r + 1) * LANES, :] = acc_ref[r * LANES:(r + 1) * LANES, :].astype(BF16)
            continue
        for r0 in range(r * LANES, (r + 1) * LANES, HEAD_DIM):
            x1 = acc_ref[r0:r0 + ROPE_HALF, :]
            x2 = acc_ref[r0 + ROPE_HALF:r0 + ROPE_DIM, :]
            rot = jnp.concatenate([x1 * ct - x2 * st, x2 * ct + x1 * st], axis=0)
            qvt_ref[r0:r0 + ROPE_DIM, :] = rot.astype(BF16)
            qvt_ref[r0 + ROPE_DIM:r0 + HEAD_DIM, :] = acc_ref[r0 + ROPE_DIM:r0 + HEAD_DIM, :].astype(BF16)


def _norm_proj(x, g, wk, wt, tabs, q_tiles):
    t, d = x.shape
    nk, nt = wk.shape[1], wt.shape[0]
    tm = ROW_TILE
    row = lambda i: (i, 0)
    col = lambda i: (0, i)
    fixed = lambda i: (0, 0)
    return pl.pallas_call(
        functools.partial(_norm_proj_kernel, q_tiles=frozenset(q_tiles)),
        grid=(t // tm,),
        in_specs=[pl.BlockSpec((tm, d), row), pl.BlockSpec((1, d), fixed),
                  pl.BlockSpec((d, nk), fixed), pl.BlockSpec((nt, d), fixed),
                  pl.BlockSpec((tm, LANES), row), pl.BlockSpec((tm, LANES), row), pl.BlockSpec((tm, LANES), row),
                  pl.BlockSpec((ROPE_HALF, tm), col), pl.BlockSpec((ROPE_HALF, tm), col)],
        out_specs=[pl.BlockSpec((tm, nk), row), pl.BlockSpec((nt, tm), col)],
        out_shape=[jax.ShapeDtypeStruct((t, nk), BF16), jax.ShapeDtypeStruct((nt, t), BF16)],
        scratch_shapes=[pltpu.VMEM((nt, tm), F32)],
        compiler_params=_cparams(("parallel",)),
        name="norm_proj_rope",
    )(x, g.reshape(1, d), wk, wt, *tabs)


def _row_masks(shape):
    row = lax.broadcasted_iota(jnp.int32, shape, 0)
    return row < HEAD_DIM, row >= HEAD_DIM


def _causal_table_t(t):
    d = np.arange(t)[None, :] - np.arange(t)[:, None]
    return jnp.asarray(np.where(d >= 0, 0.0, NEG), F32)


def _logits_stage(qt, k_of, bias_of, n_blocks, s_ref):
    t = ATT_TILE
    m = None
    for j in range(n_blocks):
        s = jnp.dot(k_of(j), qt, preferred_element_type=F32)
        bias = bias_of(j)
        if bias is not None:
            s = s + bias
        s_ref[j * t:(j + 1) * t, :] = s
        mj = jnp.max(s, axis=0, keepdims=True)
        m = mj if m is None else jnp.maximum(m, mj)
    return m


def _probs_stage(m, n_blocks, s_ref, p_ref, want_sum):
    t = ATT_TILE
    l = None
    for j in range(n_blocks):
        p = jnp.exp2(s_ref[j * t:(j + 1) * t, :] - m)
        p_ref[j * t:(j + 1) * t, :] = p.astype(BF16)
        if want_sum:
            lj = jnp.sum(p, axis=0, keepdims=True)
            l = lj if l is None else l + lj
    return l


def _values_stage(l, vt_of, n_blocks, p_ref, ones_row):
    n = n_blocks * ATT_TILE
    acc = jnp.dot(vt_of(n), p_ref[:n, :], preferred_element_type=F32)
    if l is None:
        l = acc[ones_row:ones_row + 1, :]
    return acc * (1.0 / l)


def _pipelined_attention(n_tiles, make_query, k_of_head, bias_of_tile, vt_of_head, ones_rows, finish_tile,
                         s_refs, p_refs):
    items = [(i, h) for i in range(n_tiles) for h in range(2)]
    n = len(items)
    ms, ls, outs = {}, {}, {}
    for step in range(n + 2):
        if step < n:
            i, h = items[step]
            ms[step] = _logits_stage(make_query(i, h), k_of_head(h), bias_of_tile(i), i + 1, s_refs.at[step % 2])
        if 1 <= step <= n:
            c = step - 1
            i, h = items[c]
            ls[c] = _probs_stage(ms.pop(c), i + 1, s_refs.at[c % 2], p_refs.at[c % 2], ones_rows[h] is None)
        if 2 <= step <= n + 1:
            c = step - 2
            i, h = items[c]
            outs[c] = _values_stage(ls.pop(c), vt_of_head(h), i + 1, p_refs.at[c % 2], ones_rows[h])
            if h == 1:
                finish_tile(i, outs.pop(c - 1), outs.pop(c))


def _values_with_ones(vt_ref, vta_ref, vtb_ref):
    vt = vt_ref[...].astype(F32)
    lo, hi = _row_masks(vt.shape)
    vta_ref[...] = jnp.where(lo, vt, 1.0).astype(BF16)
    vtb_ref[...] = jnp.where(hi, vt, 1.0).astype(BF16)
    vt_of = (lambda n: vta_ref[:, :n], lambda n: vtb_ref[:, :n])
    return (lambda h: vt_of[h]), (HEAD_DIM, 0)


def _key_blocks(ref):
    t = ATT_TILE
    if len(ref.shape) == 2:
        return lambda j: ref[j * t:(j + 1) * t, :]
    return lambda j: ref[0, j * t:(j + 1) * t, :]


def _attn_scratch(s):
    return [pltpu.VMEM((2, s, ATT_TILE), F32), pltpu.VMEM((2, s, ATT_TILE), BF16)]


def _k_spec(s, tile0):
    return pl.BlockSpec((1, s, LANES), lambda b, p: (b, 0, tile0 + p))


def _t_spec(s, tile0):
    return pl.BlockSpec((LANES, s), lambda b, p: (tile0 + p, b))


def _dilated_bias_tables_t(t):
    windows = sorted(w for w, _ in DILATED_CONFIGS)
    far = windows[-2] // t + 1
    tabs = []
    for delta in range(far + 1):
        d = delta * t + np.arange(t)[None, :] - np.arange(t)[:, None]
        mult = np.zeros((t, t), np.float64)
        for window, dil in DILATED_CONFIGS:
            mult += (d >= 0) & (d % dil == 0) & (d <= window)
        tabs.append(np.where(mult > 0, np.log2(np.maximum(mult, 1.0)), NEG))
    return jnp.asarray(np.stack(tabs), F32)


def _dilated_kernel(qt_ref, k_ref, vt_ref, bias_ref, o_ref, vta_ref, vtb_ref, s_refs, p_refs):
    t = ATT_TILE
    n_tab = bias_ref.shape[0]
    k_of = _key_blocks(k_ref)
    vt_of_head, ones_rows = _values_with_ones(vt_ref, vta_ref, vtb_ref)

    def make_query(i, h):
        qt = qt_ref[:, i * t:(i + 1) * t].astype(F32)
        return jnp.where(_row_masks(qt.shape)[h], qt, 0.0).astype(BF16)

    def finish_tile(i, out_lo, out_hi):
        lo, _ = _row_masks(out_lo.shape)
        o_ref[0, i * t:(i + 1) * t, :] = jnp.where(lo, out_lo, out_hi).T.astype(BF16)

    _pipelined_attention(qt_ref.shape[1] // t, make_query, lambda h: k_of,
                         lambda i: (lambda j: bias_ref[min(i - j, n_tab - 1)]),
                         vt_of_head, ones_rows, finish_tile, s_refs, p_refs)


def _dilated_attention(kn, qvt, b, s, q_tile0, k_tile0, v_tile0, n_pairs):
    bias = _dilated_bias_tables_t(ATT_TILE)
    return pl.pallas_call(
        _dilated_kernel,
        grid=(b, n_pairs),
        in_specs=[_t_spec(s, q_tile0), _k_spec(s, k_tile0), _t_spec(s, v_tile0),
                  pl.BlockSpec(bias.shape, lambda b_, p: (0, 0, 0))],
        out_specs=_k_spec(s, 0),
        out_shape=jax.ShapeDtypeStruct((b, s, n_pairs * LANES), BF16),
        scratch_shapes=[pltpu.VMEM((LANES, s), BF16), pltpu.VMEM((LANES, s), BF16)] + _attn_scratch(s),
        compiler_params=_cparams(("parallel", "parallel")),
        name="dilated_attention",
    )(qvt, kn, qvt, bias)


def _moba_kernel(qt_ref, k_ref, vt_ref, causal_ref, o_ref, ka_ref, kb_ref, vta_ref, vtb_ref, s_refs, p_refs):
    t = MOBA_BLOCK
    vt_of_head, ones_rows = _values_with_ones(vt_ref, vta_ref, vtb_ref)
    s_len = k_ref.shape[1]
    nblk = s_len // t

    k = k_ref[0].astype(F32)
    lane_k = lax.broadcasted_iota(jnp.int32, k.shape, 1)
    blk = lax.broadcasted_iota(jnp.int32, k.shape, 0) // t
    ka_ref[...] = jnp.where(lane_k < HEAD_DIM, k, jnp.where(lane_k - HEAD_DIM == blk, 1.0, 0.0)).astype(BF16)
    kb_ref[...] = jnp.where(lane_k >= HEAD_DIM, k, jnp.where(lane_k == blk, 1.0, 0.0)).astype(BF16)
    means = jnp.concatenate(
        [jnp.mean(k_ref[0, n * t:(n + 1) * t, :].astype(F32), axis=0, keepdims=True) for n in range(nblk)], axis=0)
    lane_m = lax.broadcasted_iota(jnp.int32, means.shape, 1)
    means2 = jnp.concatenate([jnp.where(lane_m < HEAD_DIM, means, 0.0),
                              jnp.where(lane_m >= HEAD_DIM, means, 0.0)], axis=0)

    zeros = jnp.zeros((HEAD_DIM - nblk, t), F32)
    k_of = (_key_blocks(ka_ref), _key_blocks(kb_ref))

    def make_query(i, h):
        qt = qt_ref[:, i * t:(i + 1) * t].astype(F32)
        if i <= MOBA_TOPK:
            return jnp.where(_row_masks(qt.shape)[h], qt, 0.0).astype(BF16)
        g = jnp.dot(means2[h * nblk:(h + 1) * nblk], qt, preferred_element_type=F32,
                    precision=lax.Precision.HIGHEST)
        row = lax.broadcasted_iota(jnp.int32, g.shape, 0)
        rank = jnp.zeros(g.shape, F32)
        for kk in range(i):
            gk = g[kk:kk + 1, :]
            beats = (gk > g) | ((gk == g) & (row > kk))
            rank = rank + jnp.where(beats, 1.0, 0.0)
        drop = jnp.where((row < i) & (rank >= MOBA_TOPK), NEG, 0.0)
        parts = [qt[:HEAD_DIM], drop, zeros] if h == 0 else [drop, zeros, qt[HEAD_DIM:]]
        return jnp.concatenate(parts, axis=0).astype(BF16)

    def finish_tile(i, out_lo, out_hi):
        lo, _ = _row_masks(out_lo.shape)
        o_ref[0, i * t:(i + 1) * t, :] = jnp.where(lo, out_lo, out_hi).T.astype(BF16)

    _pipelined_attention(nblk, make_query, lambda h: k_of[h],
                         lambda i: (lambda j: causal_ref[...] if j == i else None),
                         vt_of_head, ones_rows, finish_tile, s_refs, p_refs)


def _moba_attention(kn, qvt, b, s, q_tile0, k_tile0, v_tile0, n_pairs):
    t = MOBA_BLOCK
    assert t == ATT_TILE and s // t <= SUBLANES
    return pl.pallas_call(
        _moba_kernel,
        grid=(b, n_pairs),
        in_specs=[_t_spec(s, q_tile0), _k_spec(s, k_tile0), _t_spec(s, v_tile0),
                  pl.BlockSpec((t, t), lambda b_, p: (0, 0))],
        out_specs=_k_spec(s, 0),
        out_shape=jax.ShapeDtypeStruct((b, s, n_pairs * LANES), BF16),
        scratch_shapes=[pltpu.VMEM((s, LANES), BF16), pltpu.VMEM((s, LANES), BF16),
                        pltpu.VMEM((LANES, s), BF16), pltpu.VMEM((LANES, s), BF16)] + _attn_scratch(s),
        compiler_params=_cparams(("parallel", "parallel")),
        name="moba_attention",
    )(qvt, kn, qvt, _causal_table_t(t))


def _diff_kernel(qt_ref, k_ref, vt_ref, causal_ref, lq1_ref, lk1_ref, lq2_ref, lk2_ref, g_ref, o_ref,
                 s_refs, p_refs, *, lambda_init):
    t = ATT_TILE
    k_of = _key_blocks(k_ref)
    lam = (jnp.exp(jnp.sum(lq1_ref[...] * lk1_ref[...], axis=-1, keepdims=True))
           - jnp.exp(jnp.sum(lq2_ref[...] * lk2_ref[...], axis=-1, keepdims=True)) + lambda_init)

    def make_query(i, h):
        qt = qt_ref[:, i * t:(i + 1) * t].astype(F32)
        return jnp.where(_row_masks(qt.shape)[h], qt, 0.0).astype(BF16)

    def finish_tile(i, out1, out2):
        o = out1 - lam * out2
        y = o * lax.rsqrt(jnp.mean(o * o, axis=0, keepdims=True) + NORM_EPS)
        o_ref[0, i * t:(i + 1) * t, :] = ((y.T * g_ref[...]) * (1.0 - lambda_init)).astype(BF16)

    _pipelined_attention(qt_ref.shape[1] // t, make_query, lambda h: k_of,
                         lambda i: (lambda j: causal_ref[...] if j == i else None),
                         lambda h: (lambda n: vt_ref[:, :n]), (None, None), finish_tile, s_refs, p_refs)


def _diff_attention(kn, qvt, b, s, lq1, lk1, lq2, lk2, subln_g, lambda_init, n_heads):
    t = ATT_TILE
    vec = lambda a: a.reshape(1, -1).astype(F32)
    small = lambda n: pl.BlockSpec((1, n), lambda b_, h: (0, 0))
    return pl.pallas_call(
        functools.partial(_diff_kernel, lambda_init=lambda_init),
        grid=(b, n_heads),
        in_specs=[_t_spec(s, 0), _k_spec(s, 0), _t_spec(s, n_heads), pl.BlockSpec((t, t), lambda b_, h: (0, 0)),
                  small(HEAD_DIM), small(HEAD_DIM), small(HEAD_DIM), small(HEAD_DIM), small(LANES)],
        out_specs=_k_spec(s, 0),
        out_shape=jax.ShapeDtypeStruct((b, s, n_heads * LANES), BF16),
        scratch_shapes=_attn_scratch(s),
        compiler_params=_cparams(("parallel", "parallel")),
        name="diff_attention",
    )(qvt, kn, qvt, _causal_table_t(t), vec(lq1), vec(lk1), vec(lq2), vec(lk2), vec(subln_g))


def _mixer_out_ffn_kernel(*refs, n_acts, chunk, final_norm):
    h_ref = refs[0]
    act_refs = refs[1:1 + n_acts]
    wo_ref, g_ref, wg_ref, wu_ref, wd_ref, gf_ref, o_ref = refs[1 + n_acts:]
    act = jnp.concatenate([a_ref[...] for a_ref in act_refs], axis=1)
    x = h_ref[...] + jnp.dot(act, wo_ref[...], preferred_element_type=F32)
    ms = jnp.mean(x * x, axis=-1, keepdims=True)
    xn = (x * lax.rsqrt(ms + NORM_EPS) * g_ref[...]).astype(BF16)
    acc = x
    for c in range(wg_ref.shape[1] // chunk):
        sl = slice(c * chunk, (c + 1) * chunk)
        gate = jnp.dot(xn, wg_ref[:, sl], preferred_element_type=F32)
        up = jnp.dot(xn, wu_ref[:, sl], preferred_element_type=F32)
        mid = (gate * jax.nn.sigmoid(gate) * up).astype(BF16)
        acc = acc + jnp.dot(mid, wd_ref[sl, :], preferred_element_type=F32)
    if final_norm:
        ms = jnp.mean(acc * acc, axis=-1, keepdims=True)
        acc = acc * lax.rsqrt(ms + NORM_EPS) * gf_ref[...]
    o_ref[...] = acc


def _mixer_out_ffn(h, acts, w_out, g, wg, wu, wd, g_final, final_norm):
    t, d = h.shape
    ff = wg.shape[1]
    tm = ROW_TILE
    row = lambda i: (i, 0)
    fixed = lambda i: (0, 0)
    once = dict(pipeline_mode=pl.Buffered(1))
    in_specs = [pl.BlockSpec((tm, d), row)] + [pl.BlockSpec((tm, a.shape[1]), row) for a in acts]
    in_specs += [pl.BlockSpec(w_out.shape, fixed, **once), pl.BlockSpec((1, d), fixed),
                 pl.BlockSpec((d, ff), fixed, **once), pl.BlockSpec((d, ff), fixed, **once),
                 pl.BlockSpec((ff, d), fixed, **once), pl.BlockSpec((1, d), fixed)]
    args = [h, *acts, w_out, g.reshape(1, d), wg, wu, wd, g_final.reshape(1, d)]
    return pl.pallas_call(
        functools.partial(_mixer_out_ffn_kernel, n_acts=len(acts), chunk=256, final_norm=final_norm),
        grid=(t // tm,),
        in_specs=in_specs,
        out_specs=pl.BlockSpec((tm, d), row),
        out_shape=jax.ShapeDtypeStruct((t, d), F32),
        compiler_params=_cparams(("parallel",)),
        name="mixer_out_ffn",
    )(*args)


def _lambda_init(layer_idx):
    return 0.8 - 0.6 * math.exp(-0.3 * layer_idx)


def _split_weights(w_in, q_cols, k_cols, v_cols):
    cat = lambda cols: jnp.concatenate([w_in[:, a:b] for a, b in cols], axis=1)
    groups = [(cat([c]) * (ATTN_SCALE * LOG2E) if is_q else cat([c])) for c, is_q in
              sorted([(c, True) for c in q_cols] + [(c, False) for c in v_cols])]
    return cat(k_cols).astype(BF16), jnp.concatenate(groups, axis=1).T.astype(BF16)


def kernel(x, positions, ab_norm_g, ab_w_in, ab_w_out, diff_norm_g, diff_w_in, diff_w_out, diff_lambda_q1, diff_lambda_k1, diff_lambda_q2, diff_lambda_k2, diff_subln_g, ffn_norm_g, ffn_w_gate, ffn_w_up, ffn_w_down, final_norm_g):
    b, s, d = x.shape
    t = b * s
    n_pairs = d // (4 * HEAD_DIM)
    n_diff = d // (2 * HEAD_DIM)
    wa = n_pairs * LANES
    tabs = _rope_tables(positions)
    h = x.reshape(t, d)

    wk, wt = _split_weights(ab_w_in[0], q_cols=[(0, wa), (3 * wa, 4 * wa)],
                            k_cols=[(wa, 2 * wa), (4 * wa, 5 * wa)], v_cols=[(2 * wa, 3 * wa), (5 * wa, 6 * wa)])
    q_tiles0 = list(range(n_pairs)) + list(range(2 * n_pairs, 3 * n_pairs))
    kn, qvt = _norm_proj(h, ab_norm_g[0], wk, wt, tabs, q_tiles0)
    kn = kn.reshape(b, s, -1)
    oa = _dilated_attention(kn, qvt, b, s, 0, 0, n_pairs, n_pairs)
    ob = _moba_attention(kn, qvt, b, s, 2 * n_pairs, n_pairs, 3 * n_pairs, n_pairs)
    h = _mixer_out_ffn(h, [oa.reshape(t, wa), ob.reshape(t, wa)], ab_w_out[0].astype(BF16), ffn_norm_g[0], ffn_w_gate[0].astype(BF16), ffn_w_up[0].astype(BF16),
                       ffn_w_down[0].astype(BF16), final_norm_g, False)

    wk, wt = _split_weights(diff_w_in[0], q_cols=[(0, d)], k_cols=[(d, 2 * d)], v_cols=[(2 * d, 3 * d)])
    kn, qvt = _norm_proj(h, diff_norm_g[0], wk, wt, tabs, list(range(n_diff)))
    od = _diff_attention(kn.reshape(b, s, -1), qvt, b, s, diff_lambda_q1[0], diff_lambda_k1[0],
                         diff_lambda_q2[0], diff_lambda_k2[0], diff_subln_g[0], _lambda_init(1), n_diff)
    h = _mixer_out_ffn(h, [od.reshape(t, d)], diff_w_out[0].astype(BF16), ffn_norm_g[1], ffn_w_gate[1].astype(BF16), ffn_w_up[1].astype(BF16),
                       ffn_w_down[1].astype(BF16), final_norm_g, True)
    return h.reshape(b, s, d)
```

```python
import functools
import math

import numpy as np
import jax
import jax.numpy as jnp
from jax import lax
from jax.experimental import pallas as pl
from jax.experimental.pallas import tpu as pltpu

D_MODEL = 1024
HEAD_DIM = 64
LANES = 128
SUBLANES = 8
ROPE_DIM = HEAD_DIM // 4
ROPE_HALF = ROPE_DIM // 2
ROPE_THETA = 500000.0
D_FF = 2816
NORM_EPS = 1e-5
ATTN_SCALE = HEAD_DIM ** -0.5
LOG2E = math.log2(math.e)
NEG = -1e30
DILATED_CONFIGS = ((128, 1), (512, 4), (2048, 16))
MOBA_BLOCK = 256
MOBA_TOPK = 3

ATT_TILE = 256
ROW_TILE = 512
PROJ_ROW_TILE = 1024
VMEM_LIMIT = 56 * 1024 * 1024

F32 = jnp.float32
BF16 = jnp.bfloat16
_NT = (((1,), (1,)), ((), ()))

assert ROPE_HALF == SUBLANES


def _cparams(sem):
    return pltpu.CompilerParams(dimension_semantics=sem, vmem_limit_bytes=VMEM_LIMIT)


def _rope_table_kernel(pos_ref, invf_ref, c_ref, s1_ref, s2_ref, ct_ref, st_ref):
    ang = invf_ref[...] * pos_ref[...].astype(F32)
    cos, sin = jnp.cos(ang), jnp.sin(ang)
    ct_ref[...] = cos
    st_ref[...] = sin
    tm = ang.shape[1]
    one = jnp.ones((HEAD_DIM - ROPE_DIM, tm), F32)
    zero = jnp.zeros((ROPE_HALF, tm), F32)
    zero_rest = jnp.zeros((HEAD_DIM - ROPE_DIM, tm), F32)
    c_ref[...] = jnp.concatenate([cos, cos, one] * 2, axis=0).T
    s1_ref[...] = jnp.concatenate([-sin, zero, zero_rest] * 2, axis=0).T
    s2_ref[...] = jnp.concatenate([zero, sin, zero_rest] * 2, axis=0).T


def _rope_tables(positions):
    t = positions.size
    tm = 1024
    inv_freq = ROPE_THETA ** (-jnp.arange(0, ROPE_DIM, 2, dtype=F32) / ROPE_DIM)
    tab = jax.ShapeDtypeStruct((t, LANES), F32)
    tab_t = jax.ShapeDtypeStruct((ROPE_HALF, t), F32)
    row = lambda i: (i, 0)
    col = lambda i: (0, i)
    return pl.pallas_call(
        _rope_table_kernel,
        grid=(t // tm,),
        in_specs=[pl.BlockSpec((1, tm), col), pl.BlockSpec((ROPE_HALF, 1), lambda i: (0, 0))],
        out_specs=[pl.BlockSpec((tm, LANES), row)] * 3 + [pl.BlockSpec((ROPE_HALF, tm), col)] * 2,
        out_shape=[tab] * 3 + [tab_t] * 2,
        compiler_params=_cparams(("parallel",)),
        name="rope_tables",
    )(positions.reshape(1, t), inv_freq.reshape(ROPE_HALF, 1))


def _norm_proj_kernel(x_ref, g_ref, wk_ref, wt_ref, c_ref, s1_ref, s2_ref, ct_ref, st_ref,
                      k_ref, qvt_ref, acc_ref, *, q_tiles):
    x = x_ref[...]
    ms = jnp.mean(x * x, axis=-1, keepdims=True)
    xn = (x * lax.rsqrt(ms + NORM_EPS) * g_ref[...]).astype(BF16)

    c, s1, s2 = c_ref[...], s1_ref[...], s2_ref[...]
    chunk = 512
    for n in range(wk_ref.shape[1] // chunk):
        acc = jnp.dot(xn, wk_ref[:, n * chunk:(n + 1) * chunk], preferred_element_type=F32)
        for t in range(chunk // LANES):
            seg = acc[:, t * LANES:(t + 1) * LANES]
            seg = seg * c + pltpu.roll(seg, LANES - ROPE_HALF, 1) * s1 + pltpu.roll(seg, ROPE_HALF, 1) * s2
            k_ref[n * (chunk // LANES) + t] = seg.astype(BF16)

    acc_ref[...] = lax.dot_general(wt_ref[...], xn, _NT, preferred_element_type=F32)
    ct, st = ct_ref[...], st_ref[...]
    for r in range(qvt_ref.shape[0] // LANES):
        if r not in q_tiles:
            qvt_ref[r * LANES:(r + 1) * LANES, :] = acc_ref[r * LANES:(r + 1) * LANES, :].astype(BF16)
            continue
        for r0 in range(r * LANES, (r + 1) * LANES, HEAD_DIM):
            x1 = acc_ref[r0:r0 + ROPE_HALF, :]
            x2 = acc_ref[r0 + ROPE_HALF:r0 + ROPE_DIM, :]
            rot = jnp.concatenate([x1 * ct - x2 * st, x2 * ct + x1 * st], axis=0)
            qvt_ref[r0:r0 + ROPE_DIM, :] = rot.astype(BF16)
            qvt_ref[r0 + ROPE_DIM:r0 + HEAD_DIM, :] = acc_ref[r0 + ROPE_DIM:r0 + HEAD_DIM, :].astype(BF16)


def _norm_proj(x, g, wk, wt, tabs, q_tiles):
    t, d = x.shape
    nk, nt = wk.shape[1], wt.shape[0]
    tm = PROJ_ROW_TILE
    row = lambda i: (i, 0)
    col = lambda i: (0, i)
    fixed = lambda i: (0, 0)
    once = dict(pipeline_mode=pl.Buffered(1))
    return pl.pallas_call(
        functools.partial(_norm_proj_kernel, q_tiles=frozenset(q_tiles)),
        grid=(t // tm,),
        in_specs=[pl.BlockSpec((tm, d), row), pl.BlockSpec((1, d), fixed),
                  pl.BlockSpec((d, nk), fixed, **once), pl.BlockSpec((nt, d), fixed, **once),
                  pl.BlockSpec((tm, LANES), row), pl.BlockSpec((tm, LANES), row), pl.BlockSpec((tm, LANES), row),
                  pl.BlockSpec((ROPE_HALF, tm), col), pl.BlockSpec((ROPE_HALF, tm), col)],
        out_specs=[pl.BlockSpec((nk // LANES, tm, LANES), lambda i: (0, i, 0)), pl.BlockSpec((nt, tm), col)],
        out_shape=[jax.ShapeDtypeStruct((nk // LANES, t, LANES), BF16), jax.ShapeDtypeStruct((nt, t), BF16)],
        scratch_shapes=[pltpu.VMEM((nt, tm), F32)],
        compiler_params=_cparams(("parallel",)),
        name="norm_proj_rope",
    )(x, g.reshape(1, d), wk, wt, *tabs)


def _row_masks(shape):
    row = lax.broadcasted_iota(jnp.int32, shape, 0)
    return row < HEAD_DIM, row >= HEAD_DIM


def _causal_table_t(t):
    d = np.arange(t)[None, :] - np.arange(t)[:, None]
    return jnp.asarray(np.where(d >= 0, 0.0, NEG), F32)


def _logits_stage(qt, k_of, bias_of, n_blocks, s_ref):
    t = ATT_TILE
    m = None
    for j in range(n_blocks):
        s = jnp.dot(k_of(j), qt, preferred_element_type=F32)
        bias = bias_of(j)
        if bias is not None:
            s = s + bias
        s_ref[j * t:(j + 1) * t, :] = s
        mj = jnp.max(s, axis=0, keepdims=True)
        m = mj if m is None else jnp.maximum(m, mj)
    return m


def _probs_stage(m, n_blocks, s_ref, p_ref, want_sum):
    t = ATT_TILE
    l = None
    for j in range(n_blocks):
        p = jnp.exp2(s_ref[j * t:(j + 1) * t, :] - m)
        p_ref[j * t:(j + 1) * t, :] = p.astype(BF16)
        if want_sum:
            lj = jnp.sum(p, axis=0, keepdims=True)
            l = lj if l is None else l + lj
    return l


def _values_stage(l, vt_of, n_blocks, p_ref, ones_row):
    n = n_blocks * ATT_TILE
    acc = jnp.dot(vt_of(n), p_ref[:n, :], preferred_element_type=F32)
    if l is None:
        l = acc[ones_row:ones_row + 1, :]
    return acc * (1.0 / l)


def _pipelined_attention(n_tiles, make_query, k_of_head, bias_of_tile, vt_of_head, ones_rows, finish_tile,
                         s_refs, p_refs):
    items = [(i, h) for i in range(n_tiles) for h in range(2)]
    n = len(items)
    ms, ls, outs = {}, {}, {}
    for step in range(n + 2):
        if 1 <= step <= n:
            c = step - 1
            i, h = items[c]
            ls[c] = _probs_stage(ms.pop(c), i + 1, s_refs.at[c % 2], p_refs.at[c % 2], ones_rows[h] is None)
        if step < n:
            i, h = items[step]
            ms[step] = _logits_stage(make_query(i, h), k_of_head(h), bias_of_tile(i), i + 1, s_refs.at[step % 2])
        if 2 <= step <= n + 1:
            c = step - 2
            i, h = items[c]
            outs[c] = _values_stage(ls.pop(c), vt_of_head(h), i + 1, p_refs.at[c % 2], ones_rows[h])
            if h == 1:
                finish_tile(i, outs.pop(c - 1), outs.pop(c))


def _values_with_ones(vt_ref, vta_ref, vtb_ref):
    vt = vt_ref[...].astype(F32)
    lo, hi = _row_masks(vt.shape)
    vta_ref[...] = jnp.where(lo, vt, 1.0).astype(BF16)
    vtb_ref[...] = jnp.where(hi, vt, 1.0).astype(BF16)
    vt_of = (lambda n: vta_ref[:, :n], lambda n: vtb_ref[:, :n])
    return (lambda h: vt_of[h]), (HEAD_DIM, 0)


def _key_blocks(ref):
    t = ATT_TILE
    if len(ref.shape) == 2:
        return lambda j: ref[j * t:(j + 1) * t, :]
    return lambda j: ref[0, j * t:(j + 1) * t, :]


def _attn_scratch(s):
    return [pltpu.VMEM((2, s, ATT_TILE), F32), pltpu.VMEM((2, s, ATT_TILE), BF16)]


def _k_spec(s, tile0):
    return pl.BlockSpec((1, s, LANES), lambda b, p: (tile0 + p, b, 0))


def _t_spec(s, tile0):
    return pl.BlockSpec((LANES, s), lambda b, p: (tile0 + p, b))


def _dilated_bias_tables_t(t):
    windows = sorted(w for w, _ in DILATED_CONFIGS)
    far = windows[-2] // t + 1
    tabs = []
    for delta in range(far + 1):
        d = delta * t + np.arange(t)[None, :] - np.arange(t)[:, None]
        mult = np.zeros((t, t), np.float64)
        for window, dil in DILATED_CONFIGS:
            mult += (d >= 0) & (d % dil == 0) & (d <= window)
        tabs.append(np.where(mult > 0, np.log2(np.maximum(mult, 1.0)), NEG))
    return jnp.asarray(np.stack(tabs), F32)


def _dilated_kernel(qt_ref, k_ref, vt_ref, bias_ref, o_ref, vta_ref, vtb_ref, s_refs, p_refs):
    t = ATT_TILE
    n_tab = bias_ref.shape[0]
    k_of = _key_blocks(k_ref)
    vt_of_head, ones_rows = _values_with_ones(vt_ref, vta_ref, vtb_ref)

    def make_query(i, h):
        qt = qt_ref[:, i * t:(i + 1) * t].astype(F32)
        return jnp.where(_row_masks(qt.shape)[h], qt, 0.0).astype(BF16)

    def finish_tile(i, out_lo, out_hi):
        lo, _ = _row_masks(out_lo.shape)
        o_ref[0, i * t:(i + 1) * t, :] = jnp.where(lo, out_lo, out_hi).T.astype(BF16)

    _pipelined_attention(qt_ref.shape[1] // t, make_query, lambda h: k_of,
                         lambda i: (lambda j: bias_ref[min(i - j, n_tab - 1)]),
                         vt_of_head, ones_rows, finish_tile, s_refs, p_refs)


def _dilated_attention(kn, qvt, b, s, q_tile0, k_tile0, v_tile0, n_pairs):
    bias = _dilated_bias_tables_t(ATT_TILE)
    return pl.pallas_call(
        _dilated_kernel,
        grid=(b, n_pairs),
        in_specs=[_t_spec(s, q_tile0), _k_spec(s, k_tile0), _t_spec(s, v_tile0),
                  pl.BlockSpec(bias.shape, lambda b_, p: (0, 0, 0))],
        out_specs=_k_spec(s, 0),
        out_shape=jax.ShapeDtypeStruct((n_pairs, b * s, LANES), BF16),
        scratch_shapes=[pltpu.VMEM((LANES, s), BF16), pltpu.VMEM((LANES, s), BF16)] + _attn_scratch(s),
        compiler_params=_cparams(("parallel", "parallel")),
        name="dilated_attention",
    )(qvt, kn, qvt, bias)


def _moba_kernel(qt_ref, k_ref, vt_ref, causal_ref, o_ref, ka_ref, kb_ref, vta_ref, vtb_ref, s_refs, p_refs):
    t = MOBA_BLOCK
    vt_of_head, ones_rows = _values_with_ones(vt_ref, vta_ref, vtb_ref)
    s_len = k_ref.shape[1]
    nblk = s_len // t

    k = k_ref[0].astype(F32)
    lane_k = lax.broadcasted_iota(jnp.int32, k.shape, 1)
    blk = lax.broadcasted_iota(jnp.int32, k.shape, 0) // t
    ka_ref[...] = jnp.where(lane_k < HEAD_DIM, k, jnp.where(lane_k - HEAD_DIM == blk, 1.0, 0.0)).astype(BF16)
    kb_ref[...] = jnp.where(lane_k >= HEAD_DIM, k, jnp.where(lane_k == blk, 1.0, 0.0)).astype(BF16)
    means = jnp.concatenate(
        [jnp.mean(k_ref[0, n * t:(n + 1) * t, :].astype(F32), axis=0, keepdims=True) for n in range(nblk)], axis=0)
    lane_m = lax.broadcasted_iota(jnp.int32, means.shape, 1)
    means2 = jnp.concatenate([jnp.where(lane_m < HEAD_DIM, means, 0.0),
                              jnp.where(lane_m >= HEAD_DIM, means, 0.0)], axis=0)

    zeros = jnp.zeros((HEAD_DIM - nblk, t), F32)
    k_of = (_key_blocks(ka_ref), _key_blocks(kb_ref))

    def make_query(i, h):
        qt = qt_ref[:, i * t:(i + 1) * t].astype(F32)
        if i <= MOBA_TOPK:
            return jnp.where(_row_masks(qt.shape)[h], qt, 0.0).astype(BF16)
        g = jnp.dot(means2[h * nblk:(h + 1) * nblk], qt, preferred_element_type=F32,
                    precision=lax.Precision.HIGHEST)
        row = lax.broadcasted_iota(jnp.int32, g.shape, 0)
        rank = jnp.zeros(g.shape, F32)
        for kk in range(i):
            gk = g[kk:kk + 1, :]
            beats = (gk > g) | ((gk == g) & (row > kk))
            rank = rank + jnp.where(beats, 1.0, 0.0)
        drop = jnp.where((row < i) & (rank >= MOBA_TOPK), NEG, 0.0)
        parts = [qt[:HEAD_DIM], drop, zeros] if h == 0 else [drop, zeros, qt[HEAD_DIM:]]
        return jnp.concatenate(parts, axis=0).astype(BF16)

    def finish_tile(i, out_lo, out_hi):
        lo, _ = _row_masks(out_lo.shape)
        o_ref[0, i * t:(i + 1) * t, :] = jnp.where(lo, out_lo, out_hi).T.astype(BF16)

    _pipelined_attention(nblk, make_query, lambda h: k_of[h],
                         lambda i: (lambda j: causal_ref[...] if j == i else None),
                         vt_of_head, ones_rows, finish_tile, s_refs, p_refs)


def _moba_attention(kn, qvt, b, s, q_tile0, k_tile0, v_tile0, n_pairs):
    t = MOBA_BLOCK
    assert t == ATT_TILE and s // t <= SUBLANES
    return pl.pallas_call(
        _moba_kernel,
        grid=(b, n_pairs),
        in_specs=[_t_spec(s, q_tile0), _k_spec(s, k_tile0), _t_spec(s, v_tile0),
                  pl.BlockSpec((t, t), lambda b_, p: (0, 0))],
        out_specs=_k_spec(s, 0),
        out_shape=jax.ShapeDtypeStruct((n_pairs, b * s, LANES), BF16),
        scratch_shapes=[pltpu.VMEM((s, LANES), BF16), pltpu.VMEM((s, LANES), BF16),
                        pltpu.VMEM((LANES, s), BF16), pltpu.VMEM((LANES, s), BF16)] + _attn_scratch(s),
        compiler_params=_cparams(("parallel", "parallel")),
        name="moba_attention",
    )(qvt, kn, qvt, _causal_table_t(t))


def _diff_kernel(qt_ref, k_ref, vt_ref, causal_ref, lq1_ref, lk1_ref, lq2_ref, lk2_ref, g_ref, o_ref,
                 s_refs, p_refs, *, lambda_init):
    t = ATT_TILE
    k_of = _key_blocks(k_ref)
    lam = (jnp.exp(jnp.sum(lq1_ref[...] * lk1_ref[...], axis=-1, keepdims=True))
           - jnp.exp(jnp.sum(lq2_ref[...] * lk2_ref[...], axis=-1, keepdims=True)) + lambda_init)

    def make_query(i, h):
        qt = qt_ref[:, i * t:(i + 1) * t].astype(F32)
        return jnp.where(_row_masks(qt.shape)[h], qt, 0.0).astype(BF16)

    def finish_tile(i, out1, out2):
        o = out1 - lam * out2
        y = o * lax.rsqrt(jnp.mean(o * o, axis=0, keepdims=True) + NORM_EPS)
        o_ref[0, i * t:(i + 1) * t, :] = ((y.T * g_ref[...]) * (1.0 - lambda_init)).astype(BF16)

    _pipelined_attention(qt_ref.shape[1] // t, make_query, lambda h: k_of,
                         lambda i: (lambda j: causal_ref[...] if j == i else None),
                         lambda h: (lambda n: vt_ref[:, :n]), (None, None), finish_tile, s_refs, p_refs)


def _diff_attention(kn, qvt, b, s, lq1, lk1, lq2, lk2, subln_g, lambda_init, n_heads):
    t = ATT_TILE
    vec = lambda a: a.reshape(1, -1).astype(F32)
    small = lambda n: pl.BlockSpec((1, n), lambda b_, h: (0, 0))
    return pl.pallas_call(
        functools.partial(_diff_kernel, lambda_init=lambda_init),
        grid=(b, n_heads),
        in_specs=[_t_spec(s, 0), _k_spec(s, 0), _t_spec(s, n_heads), pl.BlockSpec((t, t), lambda b_, h: (0, 0)),
                  small(HEAD_DIM), small(HEAD_DIM), small(HEAD_DIM), small(HEAD_DIM), small(LANES)],
        out_specs=_k_spec(s, 0),
        out_shape=jax.ShapeDtypeStruct((n_heads, b * s, LANES), BF16),
        scratch_shapes=_attn_scratch(s),
        compiler_params=_cparams(("parallel", "parallel")),
        name="diff_attention",
    )(qvt, kn, qvt, _causal_table_t(t), vec(lq1), vec(lk1), vec(lq2), vec(lk2), vec(subln_g))


def _mixer_out_ffn_kernel(*refs, n_acts, chunk, final_norm):
    h_ref = refs[0]
    act_refs = refs[1:1 + n_acts]
    wo_ref, g_ref, wg_ref, wu_ref, wd_ref, gf_ref, o_ref = refs[1 + n_acts:]
    act = jnp.concatenate([a_ref[c] for a_ref in act_refs for c in range(a_ref.shape[0])], axis=1)
    x = h_ref[...] + jnp.dot(act, wo_ref[...], preferred_element_type=F32)
    ms = jnp.mean(x * x, axis=-1, keepdims=True)
    xn = (x * lax.rsqrt(ms + NORM_EPS) * g_ref[...]).astype(BF16)
    acc = x
    for c in range(wg_ref.shape[1] // chunk):
        sl = slice(c * chunk, (c + 1) * chunk)
        gate = jnp.dot(xn, wg_ref[:, sl], preferred_element_type=F32)
        up = jnp.dot(xn, wu_ref[:, sl], preferred_element_type=F32)
        mid = (gate * jax.nn.sigmoid(gate) * up).astype(BF16)
        acc = acc + jnp.dot(mid, wd_ref[sl, :], preferred_element_type=F32)
    if final_norm:
        ms = jnp.mean(acc * acc, axis=-1, keepdims=True)
        acc = acc * lax.rsqrt(ms + NORM_EPS) * gf_ref[...]
    o_ref[...] = acc


def _mixer_out_ffn(h, acts, w_out, g, wg, wu, wd, g_final, final_norm):
    t, d = h.shape
    ff = wg.shape[1]
    tm = ROW_TILE
    row = lambda i: (i, 0)
    fixed = lambda i: (0, 0)
    once = dict(pipeline_mode=pl.Buffered(1))
    in_specs = [pl.BlockSpec((tm, d), row)]
    in_specs += [pl.BlockSpec((a.shape[0], tm, LANES), lambda i: (0, i, 0)) for a in acts]
    in_specs += [pl.BlockSpec(w_out.shape, fixed, **once), pl.BlockSpec((1, d), fixed),
                 pl.BlockSpec((d, ff), fixed, **once), pl.BlockSpec((d, ff), fixed, **once),
                 pl.BlockSpec((ff, d), fixed, **once), pl.BlockSpec((1, d), fixed)]
    args = [h, *acts, w_out, g.reshape(1, d), wg, wu, wd, g_final.reshape(1, d)]
    return pl.pallas_call(
        functools.partial(_mixer_out_ffn_kernel, n_acts=len(acts), chunk=256, final_norm=final_norm),
        grid=(t // tm,),
        in_specs=in_specs,
        out_specs=pl.BlockSpec((tm, d), row),
        out_shape=jax.ShapeDtypeStruct((t, d), F32),
        compiler_params=_cparams(("parallel",)),
        name="mixer_out_ffn",
    )(*args)


def _lambda_init(layer_idx):
    return 0.8 - 0.6 * math.exp(-0.3 * layer_idx)


def _split_weights(w_in, q_cols, k_cols, v_cols):
    cat = lambda cols: jnp.concatenate([w_in[:, a:b] for a, b in cols], axis=1)
    groups = [(cat([c]) * (ATTN_SCALE * LOG2E) if is_q else cat([c])) for c, is_q in
              sorted([(c, True) for c in q_cols] + [(c, False) for c in v_cols])]
    return cat(k_cols).astype(BF16), jnp.concatenate(groups, axis=1).T.astype(BF16)


def kernel(x, positions, ab_norm_g, ab_w_in, ab_w_out, diff_norm_g, diff_w_in, diff_w_out, diff_lambda_q1, diff_lambda_k1, diff_lambda_q2, diff_lambda_k2, diff_subln_g, ffn_norm_g, ffn_w_gate, ffn_w_up, ffn_w_down, final_norm_g):
    b, s, d = x.shape
    t = b * s
    n_pairs = d // (4 * HEAD_DIM)
    n_diff = d // (2 * HEAD_DIM)
    wa = n_pairs * LANES
    tabs = _rope_tables(positions)
    h = x.reshape(t, d)

    wk, wt = _split_weights(ab_w_in[0], q_cols=[(0, wa), (3 * wa, 4 * wa)],
                            k_cols=[(wa, 2 * wa), (4 * wa, 5 * wa)], v_cols=[(2 * wa, 3 * wa), (5 * wa, 6 * wa)])
    q_tiles0 = list(range(n_pairs)) + list(range(2 * n_pairs, 3 * n_pairs))
    kn, qvt = _norm_proj(h, ab_norm_g[0], wk, wt, tabs, q_tiles0)
    oa = _dilated_attention(kn, qvt, b, s, 0, 0, n_pairs, n_pairs)
    ob = _moba_attention(kn, qvt, b, s, 2 * n_pairs, n_pairs, 3 * n_pairs, n_pairs)
    h = _mixer_out_ffn(h, [oa, ob], ab_w_out[0].astype(BF16), ffn_norm_g[0], ffn_w_gate[0].astype(BF16),
                       ffn_w_up[0].astype(BF16), ffn_w_down[0].astype(BF16), final_norm_g, False)

    wk, wt = _split_weights(diff_w_in[0], q_cols=[(0, d)], k_cols=[(d, 2 * d)], v_cols=[(2 * d, 3 * d)])
    kn, qvt = _norm_proj(h, diff_norm_g[0], wk, wt, tabs, list(range(n_diff)))
    od = _diff_attention(kn, qvt, b, s, diff_lambda_q1[0], diff_lambda_k1[0],
                         diff_lambda_q2[0], diff_lambda_k2[0], diff_subln_g[0], _lambda_init(1), n_diff)
    h = _mixer_out_ffn(h, [od], diff_w_out[0].astype(BF16), ffn_norm_g[1], ffn_w_gate[1].astype(BF16),
                       ffn_w_up[1].astype(BF16), ffn_w_down[1].astype(BF16), final_norm_g, True)
    return h.reshape(b, s, d)
```

```python
import functools
import math

import numpy as np
import jax
import jax.numpy as jnp
from jax import lax
from jax.experimental import pallas as pl
from jax.experimental.pallas import tpu as pltpu

D_MODEL = 1024
HEAD_DIM = 64
LANES = 128
SUBLANES = 8
ONES_SLAB = 2 * SUBLANES
ROPE_DIM = HEAD_DIM // 4
ROPE_HALF = ROPE_DIM // 2
ROPE_THETA = 500000.0
D_FF = 2816
NORM_EPS = 1e-5
ATTN_SCALE = HEAD_DIM ** -0.5
LOG2E = math.log2(math.e)
NEG = -1e30
DILATED_CONFIGS = ((128, 1), (512, 4), (2048, 16))
MOBA_BLOCK = 256
MOBA_TOPK = 3

ATT_TILE = 256
ROW_TILE = 512
PROJ_ROW_TILE = 1024
VMEM_LIMIT = 56 * 1024 * 1024

F32 = jnp.float32
BF16 = jnp.bfloat16
_NT = (((1,), (1,)), ((), ()))

assert ROPE_HALF == SUBLANES


def _cparams(sem):
    return pltpu.CompilerParams(dimension_semantics=sem, vmem_limit_bytes=VMEM_LIMIT)


def _rope_table_kernel(pos_ref, invf_ref, c_ref, s1_ref, s2_ref, ct_ref, st_ref):
    ang = invf_ref[...] * pos_ref[...].astype(F32)
    cos, sin = jnp.cos(ang), jnp.sin(ang)
    ct_ref[...] = cos
    st_ref[...] = sin
    tm = ang.shape[1]
    one = jnp.ones((HEAD_DIM - ROPE_DIM, tm), F32)
    zero = jnp.zeros((ROPE_HALF, tm), F32)
    zero_rest = jnp.zeros((HEAD_DIM - ROPE_DIM, tm), F32)
    c_ref[...] = jnp.concatenate([cos, cos, one] * 2, axis=0).T
    s1_ref[...] = jnp.concatenate([-sin, zero, zero_rest] * 2, axis=0).T
    s2_ref[...] = jnp.concatenate([zero, sin, zero_rest] * 2, axis=0).T


def _rope_tables(positions):
    t = positions.size
    tm = 1024
    inv_freq = ROPE_THETA ** (-jnp.arange(0, ROPE_DIM, 2, dtype=F32) / ROPE_DIM)
    tab = jax.ShapeDtypeStruct((t, LANES), F32)
    tab_t = jax.ShapeDtypeStruct((ROPE_HALF, t), F32)
    row = lambda i: (i, 0)
    col = lambda i: (0, i)
    return pl.pallas_call(
        _rope_table_kernel,
        grid=(t // tm,),
        in_specs=[pl.BlockSpec((1, tm), col), pl.BlockSpec((ROPE_HALF, 1), lambda i: (0, 0))],
        out_specs=[pl.BlockSpec((tm, LANES), row)] * 3 + [pl.BlockSpec((ROPE_HALF, tm), col)] * 2,
        out_shape=[tab] * 3 + [tab_t] * 2,
        compiler_params=_cparams(("parallel",)),
        name="rope_tables",
    )(positions.reshape(1, t), inv_freq.reshape(ROPE_HALF, 1))


def _norm_proj_kernel(x_ref, g_ref, wk_ref, wt_ref, c_ref, s1_ref, s2_ref, ct_ref, st_ref,
                      k_ref, qvt_ref, acc_ref, *, q_tiles):
    x = x_ref[...]
    ms = jnp.mean(x * x, axis=-1, keepdims=True)
    xn = (x * lax.rsqrt(ms + NORM_EPS) * g_ref[...]).astype(BF16)

    c, s1, s2 = c_ref[...], s1_ref[...], s2_ref[...]
    chunk = 512
    for n in range(wk_ref.shape[1] // chunk):
        acc = jnp.dot(xn, wk_ref[:, n * chunk:(n + 1) * chunk], preferred_element_type=F32)
        for t in range(chunk // LANES):
            seg = acc[:, t * LANES:(t + 1) * LANES]
            seg = seg * c + pltpu.roll(seg, LANES - ROPE_HALF, 1) * s1 + pltpu.roll(seg, ROPE_HALF, 1) * s2
            k_ref[n * (chunk // LANES) + t] = seg.astype(BF16)

    acc_ref[...] = lax.dot_general(wt_ref[...], xn, _NT, preferred_element_type=F32)
    ct, st = ct_ref[...], st_ref[...]
    for r in range(qvt_ref.shape[0] // LANES):
        if r not in q_tiles:
            qvt_ref[r * LANES:(r + 1) * LANES, :] = acc_ref[r * LANES:(r + 1) * LANES, :].astype(BF16)
            continue
        for r0 in range(r * LANES, (r + 1) * LANES, HEAD_DIM):
            x1 = acc_ref[r0:r0 + ROPE_HALF, :]
            x2 = acc_ref[r0 + ROPE_HALF:r0 + ROPE_DIM, :]
            rot = jnp.concatenate([x1 * ct - x2 * st, x2 * ct + x1 * st], axis=0)
            qvt_ref[r0:r0 + ROPE_DIM, :] = rot.astype(BF16)
            qvt_ref[r0 + ROPE_DIM:r0 + HEAD_DIM, :] = acc_ref[r0 + ROPE_DIM:r0 + HEAD_DIM, :].astype(BF16)


def _norm_proj(x, g, wk, wt, tabs, q_tiles):
    t, d = x.shape
    nk, nt = wk.shape[1], wt.shape[0]
    tm = PROJ_ROW_TILE
    row = lambda i: (i, 0)
    col = lambda i: (0, i)
    fixed = lambda i: (0, 0)
    once = dict(pipeline_mode=pl.Buffered(1))
    return pl.pallas_call(
        functools.partial(_norm_proj_kernel, q_tiles=frozenset(q_tiles)),
        grid=(t // tm,),
        in_specs=[pl.BlockSpec((tm, d), row), pl.BlockSpec((1, d), fixed),
                  pl.BlockSpec((d, nk), fixed, **once), pl.BlockSpec((nt, d), fixed, **once),
                  pl.BlockSpec((tm, LANES), row), pl.BlockSpec((tm, LANES), row), pl.BlockSpec((tm, LANES), row),
                  pl.BlockSpec((ROPE_HALF, tm), col), pl.BlockSpec((ROPE_HALF, tm), col)],
        out_specs=[pl.BlockSpec((nk // LANES, tm, LANES), lambda i: (0, i, 0)), pl.BlockSpec((nt, tm), col)],
        out_shape=[jax.ShapeDtypeStruct((nk // LANES, t, LANES), BF16), jax.ShapeDtypeStruct((nt, t), BF16)],
        scratch_shapes=[pltpu.VMEM((nt, tm), F32)],
        compiler_params=_cparams(("parallel",)),
        name="norm_proj_rope",
    )(x, g.reshape(1, d), wk, wt, *tabs)


def _row_masks(shape):
    row = lax.broadcasted_iota(jnp.int32, shape, 0)
    return row < HEAD_DIM, row >= HEAD_DIM


def _causal_table_t(t):
    d = np.arange(t)[None, :] - np.arange(t)[:, None]
    return jnp.asarray(np.where(d >= 0, 0.0, NEG), F32)


def _logits_stage(qt, k_of, bias_of, n_blocks, s_ref):
    t = ATT_TILE
    m = None
    for j in range(n_blocks):
        s = jnp.dot(k_of(j), qt, preferred_element_type=F32)
        bias = bias_of(j)
        if bias is not None:
            s = s + bias
        s = s.astype(BF16)
        s_ref[j * t:(j + 1) * t, :] = s
        mj = jnp.max(s, axis=0, keepdims=True)
        m = mj if m is None else jnp.maximum(m, mj)
    return m


def _probs_stage(m, n_blocks, s_ref, p_ref, want_sum):
    t = ATT_TILE
    l = None
    for j in range(n_blocks):
        p = jnp.exp2(s_ref[j * t:(j + 1) * t, :] - m)
        p_ref[j * t:(j + 1) * t, :] = p
        if want_sum:
            lj = jnp.sum(p.astype(F32), axis=0, keepdims=True)
            l = lj if l is None else l + lj
    return l


def _values_stage(l, vt_of, n_blocks, p_ref, ones_row):
    n = n_blocks * ATT_TILE
    acc = jnp.dot(vt_of(n), p_ref[:n, :], preferred_element_type=F32)
    if l is None:
        l = acc[ones_row:ones_row + 1, :]
    return acc[:LANES] * (1.0 / l)


def _pipelined_attention(n_tiles, make_query, k_of_head, bias_of_tile, vt_of_head, ones_rows, finish_tile,
                         s_refs, p_refs):
    items = [(i, h) for i in reversed(range(n_tiles)) for h in range(2)]
    n = len(items)
    ms, ls, outs = {}, {}, {}
    for step in range(n + 2):
        if 1 <= step <= n:
            c = step - 1
            i, h = items[c]
            ls[c] = _probs_stage(ms.pop(c), i + 1, s_refs.at[c % 2], p_refs.at[c % 2], ones_rows[h] is None)
        if step < n:
            i, h = items[step]
            ms[step] = _logits_stage(make_query(i, h), k_of_head(h), bias_of_tile(i), i + 1, s_refs.at[step % 2])
        if 2 <= step <= n + 1:
            c = step - 2
            i, h = items[c]
            outs[c] = _values_stage(ls.pop(c), vt_of_head(h), i + 1, p_refs.at[c % 2], ones_rows[h])
            if h == 1:
                finish_tile(i, outs.pop(c - 1), outs.pop(c))


def _values_with_ones(vt_ref, vta_ref, vtb_ref):
    vt = vt_ref[...].astype(F32)
    lo, hi = _row_masks(vt.shape)
    vta_ref[...] = jnp.where(lo, vt, 1.0).astype(BF16)
    vtb_ref[...] = jnp.where(hi, vt, 1.0).astype(BF16)
    vt_of = (lambda n: vta_ref[:, :n], lambda n: vtb_ref[:, :n])
    return (lambda h: vt_of[h]), (HEAD_DIM, 0)


def _key_blocks(ref):
    t = ATT_TILE
    if len(ref.shape) == 2:
        return lambda j: ref[j * t:(j + 1) * t, :]
    return lambda j: ref[0, j * t:(j + 1) * t, :]


def _attn_scratch(s):
    return [pltpu.VMEM((2, s, ATT_TILE), BF16), pltpu.VMEM((2, s, ATT_TILE), BF16)]


def _k_spec(s, tile0):
    return pl.BlockSpec((1, s, LANES), lambda b, p: (tile0 + p, b, 0))


def _t_spec(s, tile0):
    return pl.BlockSpec((LANES, s), lambda b, p: (tile0 + p, b))


def _dilated_bias_tables_t(t):
    windows = sorted(w for w, _ in DILATED_CONFIGS)
    far = windows[-2] // t + 1
    tabs = []
    for delta in range(far + 1):
        d = delta * t + np.arange(t)[None, :] - np.arange(t)[:, None]
        mult = np.zeros((t, t), np.float64)
        for window, dil in DILATED_CONFIGS:
            mult += (d >= 0) & (d % dil == 0) & (d <= window)
        tabs.append(np.where(mult > 0, np.log2(np.maximum(mult, 1.0)), NEG))
    return jnp.asarray(np.stack(tabs), F32)


def _dilated_kernel(qt_ref, k_ref, vt_ref, bias_ref, o_ref, vta_ref, vtb_ref, s_refs, p_refs):
    t = ATT_TILE
    n_tab = bias_ref.shape[0]
    k_of = _key_blocks(k_ref)
    vt_of_head, ones_rows = _values_with_ones(vt_ref, vta_ref, vtb_ref)

    def make_query(i, h):
        qt = qt_ref[:, i * t:(i + 1) * t].astype(F32)
        return jnp.where(_row_masks(qt.shape)[h], qt, 0.0).astype(BF16)

    def finish_tile(i, out_lo, out_hi):
        lo, _ = _row_masks(out_lo.shape)
        o_ref[0, i * t:(i + 1) * t, :] = jnp.where(lo, out_lo, out_hi).T.astype(BF16)

    _pipelined_attention(qt_ref.shape[1] // t, make_query, lambda h: k_of,
                         lambda i: (lambda j: bias_ref[min(i - j, n_tab - 1)]),
                         vt_of_head, ones_rows, finish_tile, s_refs, p_refs)


def _dilated_attention(kn, qvt, b, s, q_tile0, k_tile0, v_tile0, n_pairs):
    bias = _dilated_bias_tables_t(ATT_TILE)
    return pl.pallas_call(
        _dilated_kernel,
        grid=(b, n_pairs),
        in_specs=[_t_spec(s, q_tile0), _k_spec(s, k_tile0), _t_spec(s, v_tile0),
                  pl.BlockSpec(bias.shape, lambda b_, p: (0, 0, 0))],
        out_specs=_k_spec(s, 0),
        out_shape=jax.ShapeDtypeStruct((n_pairs, b * s, LANES), BF16),
        scratch_shapes=[pltpu.VMEM((LANES, s), BF16), pltpu.VMEM((LANES, s), BF16)] + _attn_scratch(s),
        compiler_params=_cparams(("parallel", "parallel")),
        name="dilated_attention",
    )(qvt, kn, qvt, bias)


def _moba_kernel(qt_ref, k_ref, vt_ref, causal_ref, o_ref, ka_ref, kb_ref, vta_ref, vtb_ref, s_refs, p_refs):
    t = MOBA_BLOCK
    vt_of_head, ones_rows = _values_with_ones(vt_ref, vta_ref, vtb_ref)
    s_len = k_ref.shape[1]
    nblk = s_len // t

    k = k_ref[0].astype(F32)
    lane_k = lax.broadcasted_iota(jnp.int32, k.shape, 1)
    blk = lax.broadcasted_iota(jnp.int32, k.shape, 0) // t
    ka_ref[...] = jnp.where(lane_k < HEAD_DIM, k, jnp.where(lane_k - HEAD_DIM == blk, 1.0, 0.0)).astype(BF16)
    kb_ref[...] = jnp.where(lane_k >= HEAD_DIM, k, jnp.where(lane_k == blk, 1.0, 0.0)).astype(BF16)
    means = jnp.concatenate(
        [jnp.mean(k_ref[0, n * t:(n + 1) * t, :].astype(F32), axis=0, keepdims=True) for n in range(nblk)], axis=0)
    lane_m = lax.broadcasted_iota(jnp.int32, means.shape, 1)
    means2 = jnp.concatenate([jnp.where(lane_m < HEAD_DIM, means, 0.0),
                              jnp.where(lane_m >= HEAD_DIM, means, 0.0)], axis=0)

    zeros = jnp.zeros((HEAD_DIM - nblk, t), F32)
    k_of = (_key_blocks(ka_ref), _key_blocks(kb_ref))

    def make_query(i, h):
        qt = qt_ref[:, i * t:(i + 1) * t].astype(F32)
        if i <= MOBA_TOPK:
            return jnp.where(_row_masks(qt.shape)[h], qt, 0.0).astype(BF16)
        g = jnp.dot(means2[h * nblk:(h + 1) * nblk], qt, preferred_element_type=F32,
                    precision=lax.Precision.HIGHEST)
        row = lax.broadcasted_iota(jnp.int32, g.shape, 0)
        rank = jnp.zeros(g.shape, F32)
        for kk in range(i):
            gk = g[kk:kk + 1, :]
            beats = (gk > g) | ((gk == g) & (row > kk))
            rank = rank + jnp.where(beats, 1.0, 0.0)
        drop = jnp.where((row < i) & (rank >= MOBA_TOPK), NEG, 0.0)
        parts = [qt[:HEAD_DIM], drop, zeros] if h == 0 else [drop, zeros, qt[HEAD_DIM:]]
        return jnp.concatenate(parts, axis=0).astype(BF16)

    def finish_tile(i, out_lo, out_hi):
        lo, _ = _row_masks(out_lo.shape)
        o_ref[0, i * t:(i + 1) * t, :] = jnp.where(lo, out_lo, out_hi).T.astype(BF16)

    _pipelined_attention(nblk, make_query, lambda h: k_of[h],
                         lambda i: (lambda j: causal_ref[...] if j == i else None),
                         vt_of_head, ones_rows, finish_tile, s_refs, p_refs)


def _moba_attention(kn, qvt, b, s, q_tile0, k_tile0, v_tile0, n_pairs):
    t = MOBA_BLOCK
    assert t == ATT_TILE and s // t <= SUBLANES
    return pl.pallas_call(
        _moba_kernel,
        grid=(b, n_pairs),
        in_specs=[_t_spec(s, q_tile0), _k_spec(s, k_tile0), _t_spec(s, v_tile0),
                  pl.BlockSpec((t, t), lambda b_, p: (0, 0))],
        out_specs=_k_spec(s, 0),
        out_shape=jax.ShapeDtypeStruct((n_pairs, b * s, LANES), BF16),
        scratch_shapes=[pltpu.VMEM((s, LANES), BF16), pltpu.VMEM((s, LANES), BF16),
                        pltpu.VMEM((LANES, s), BF16), pltpu.VMEM((LANES, s), BF16)] + _attn_scratch(s),
        compiler_params=_cparams(("parallel", "parallel")),
        name="moba_attention",
    )(qvt, kn, qvt, _causal_table_t(t))


def _diff_kernel(qt_ref, k_ref, vt_ref, causal_ref, lq1_ref, lk1_ref, lq2_ref, lk2_ref, g_ref, o_ref,
                 vt1_ref, s_refs, p_refs, *, lambda_init):
    t = ATT_TILE
    k_of = _key_blocks(k_ref)
    vt1_ref[:LANES, :] = vt_ref[...]
    vt1_ref[LANES:, :] = jnp.ones((ONES_SLAB, vt_ref.shape[1]), BF16)
    lam = (jnp.exp(jnp.sum(lq1_ref[...] * lk1_ref[...], axis=-1, keepdims=True))
           - jnp.exp(jnp.sum(lq2_ref[...] * lk2_ref[...], axis=-1, keepdims=True)) + lambda_init)

    def make_query(i, h):
        qt = qt_ref[:, i * t:(i + 1) * t].astype(F32)
        return jnp.where(_row_masks(qt.shape)[h], qt, 0.0).astype(BF16)

    def finish_tile(i, out1, out2):
        o = out1 - lam * out2
        y = o * lax.rsqrt(jnp.mean(o * o, axis=0, keepdims=True) + NORM_EPS)
        o_ref[0, i * t:(i + 1) * t, :] = ((y.T * g_ref[...]) * (1.0 - lambda_init)).astype(BF16)

    _pipelined_attention(qt_ref.shape[1] // t, make_query, lambda h: k_of,
                         lambda i: (lambda j: causal_ref[...] if j == i else None),
                         lambda h: (lambda n: vt1_ref[:, :n]), (LANES, LANES), finish_tile, s_refs, p_refs)


def _diff_attention(kn, qvt, b, s, lq1, lk1, lq2, lk2, subln_g, lambda_init, n_heads):
    t = ATT_TILE
    vec = lambda a: a.reshape(1, -1).astype(F32)
    small = lambda n: pl.BlockSpec((1, n), lambda b_, h: (0, 0))
    return pl.pallas_call(
        functools.partial(_diff_kernel, lambda_init=lambda_init),
        grid=(b, n_heads),
        in_specs=[_t_spec(s, 0), _k_spec(s, 0), _t_spec(s, n_heads), pl.BlockSpec((t, t), lambda b_, h: (0, 0)),
                  small(HEAD_DIM), small(HEAD_DIM), small(HEAD_DIM), small(HEAD_DIM), small(LANES)],
        out_specs=_k_spec(s, 0),
        out_shape=jax.ShapeDtypeStruct((n_heads, b * s, LANES), BF16),
        scratch_shapes=[pltpu.VMEM((LANES + ONES_SLAB, s), BF16)] + _attn_scratch(s),
        compiler_params=_cparams(("parallel", "parallel")),
        name="diff_attention",
    )(qvt, kn, qvt, _causal_table_t(t), vec(lq1), vec(lk1), vec(lq2), vec(lk2), vec(subln_g))


def _mixer_out_ffn_kernel(*refs, n_acts, chunk, final_norm):
    h_ref = refs[0]
    act_refs = refs[1:1 + n_acts]
    wo_ref, g_ref, wg_ref, wu_ref, wd_ref, gf_ref, o_ref = refs[1 + n_acts:]
    act = jnp.concatenate([a_ref[c] for a_ref in act_refs for c in range(a_ref.shape[0])], axis=1)
    x = h_ref[...] + jnp.dot(act, wo_ref[...], preferred_element_type=F32)
    ms = jnp.mean(x * x, axis=-1, keepdims=True)
    xn = (x * lax.rsqrt(ms + NORM_EPS) * g_ref[...]).astype(BF16)
    acc = x
    for c in range(wg_ref.shape[1] // chunk):
        sl = slice(c * chunk, (c + 1) * chunk)
        gate = jnp.dot(xn, wg_ref[:, sl], preferred_element_type=F32)
        up = jnp.dot(xn, wu_ref[:, sl], preferred_element_type=F32)
        mid = (gate * jax.nn.sigmoid(gate) * up).astype(BF16)
        acc = acc + jnp.dot(mid, wd_ref[sl, :], preferred_element_type=F32)
    if final_norm:
        ms = jnp.mean(acc * acc, axis=-1, keepdims=True)
        acc = acc * lax.rsqrt(ms + NORM_EPS) * gf_ref[...]
    o_ref[...] = acc


def _mixer_out_ffn(h, acts, w_out, g, wg, wu, wd, g_final, final_norm):
    t, d = h.shape
    ff = wg.shape[1]
    tm = ROW_TILE
    row = lambda i: (i, 0)
    fixed = lambda i: (0, 0)
    once = dict(pipeline_mode=pl.Buffered(1))
    in_specs = [pl.BlockSpec((tm, d), row)]
    in_specs += [pl.BlockSpec((a.shape[0], tm, LANES), lambda i: (0, i, 0)) for a in acts]
    in_specs += [pl.BlockSpec(w_out.shape, fixed, **once), pl.BlockSpec((1, d), fixed),
                 pl.BlockSpec((d, ff), fixed, **once), pl.BlockSpec((d, ff), fixed, **once),
                 pl.BlockSpec((ff, d), fixed, **once), pl.BlockSpec((1, d), fixed)]
    args = [h, *acts, w_out, g.reshape(1, d), wg, wu, wd, g_final.reshape(1, d)]
    return pl.pallas_call(
        functools.partial(_mixer_out_ffn_kernel, n_acts=len(acts), chunk=256, final_norm=final_norm),
        grid=(t // tm,),
        in_specs=in_specs,
        out_specs=pl.BlockSpec((tm, d), row),
        out_shape=jax.ShapeDtypeStruct((t, d), F32),
        compiler_params=_cparams(("parallel",)),
        name="mixer_out_ffn",
    )(*args)


def _lambda_init(layer_idx):
    return 0.8 - 0.6 * math.exp(-0.3 * layer_idx)


def _split_weights(w_in, q_cols, k_cols, v_cols):
    cat = lambda cols: jnp.concatenate([w_in[:, a:b] for a, b in cols], axis=1)
    groups = [(cat([c]) * (ATTN_SCALE * LOG2E) if is_q else cat([c])) for c, is_q in
              sorted([(c, True) for c in q_cols] + [(c, False) for c in v_cols])]
    return cat(k_cols).astype(BF16), jnp.concatenate(groups, axis=1).T.astype(BF16)


def kernel(x, positions, ab_norm_g, ab_w_in, ab_w_out, diff_norm_g, diff_w_in, diff_w_out, diff_lambda_q1, diff_lambda_k1, diff_lambda_q2, diff_lambda_k2, diff_subln_g, ffn_norm_g, ffn_w_gate, ffn_w_up, ffn_w_down, final_norm_g):
    b, s, d = x.shape
    t = b * s
    n_pairs = d // (4 * HEAD_DIM)
    n_diff = d // (2 * HEAD_DIM)
    wa = n_pairs * LANES
    tabs = _rope_tables(positions)
    h = x.reshape(t, d)

    wk, wt = _split_weights(ab_w_in[0], q_cols=[(0, wa), (3 * wa, 4 * wa)],
                            k_cols=[(wa, 2 * wa), (4 * wa, 5 * wa)], v_cols=[(2 * wa, 3 * wa), (5 * wa, 6 * wa)])
    q_tiles0 = list(range(n_pairs)) + list(range(2 * n_pairs, 3 * n_pairs))
    kn, qvt = _norm_proj(h, ab_norm_g[0], wk, wt, tabs, q_tiles0)
    oa = _dilated_attention(kn, qvt, b, s, 0, 0, n_pairs, n_pairs)
    ob = _moba_attention(kn, qvt, b, s, 2 * n_pairs, n_pairs, 3 * n_pairs, n_pairs)
    h = _mixer_out_ffn(h, [oa, ob], ab_w_out[0].astype(BF16), ffn_norm_g[0], ffn_w_gate[0].astype(BF16),
                       ffn_w_up[0].astype(BF16), ffn_w_down[0].astype(BF16), final_norm_g, False)

    wk, wt = _split_weights(diff_w_in[0], q_cols=[(0, d)], k_cols=[(d, 2 * d)], v_cols=[(2 * d, 3 * d)])
    kn, qvt = _norm_proj(h, diff_norm_g[0], wk, wt, tabs, list(range(n_diff)))
    od = _diff_attention(kn, qvt, b, s, diff_lambda_q1[0], diff_lambda_k1[0],
                         diff_lambda_q2[0], diff_lambda_k2[0], diff_subln_g[0], _lambda_init(1), n_diff)
    h = _mixer_out_ffn(h, [od], diff_w_out[0].astype(BF16), ffn_norm_g[1], ffn_w_gate[1].astype(BF16),
                       ffn_w_up[1].astype(BF16), ffn_w_down[1].astype(BF16), final_norm_g, True)
    return h.reshape(b, s, d)
```

```python
import functools
import math

import numpy as np
import jax
import jax.numpy as jnp
from jax import lax
from jax.experimental import pallas as pl
from jax.experimental.pallas import tpu as pltpu

D_MODEL = 1024
HEAD_DIM = 64
LANES = 128
SUBLANES = 8
ROPE_DIM = HEAD_DIM // 4
ROPE_HALF = ROPE_DIM // 2
ROPE_THETA = 500000.0
D_FF = 2816
NORM_EPS = 1e-5
ATTN_SCALE = HEAD_DIM ** -0.5
LOG2E = math.log2(math.e)
NEG = -1e30
DILATED_CONFIGS = ((128, 1), (512, 4), (2048, 16))
MOBA_BLOCK = 256
MOBA_TOPK = 3

ATT_TILE = 256
ROW_TILE = 512
PROJ_ROW_TILE = 1024
VMEM_LIMIT = 56 * 1024 * 1024

F32 = jnp.float32
BF16 = jnp.bfloat16
_NT = (((1,), (1,)), ((), ()))

assert ROPE_HALF == SUBLANES


def _cparams(sem):
    return pltpu.CompilerParams(dimension_semantics=sem, vmem_limit_bytes=VMEM_LIMIT)


def _rope_table_kernel(pos_ref, invf_ref, c_ref, s1_ref, s2_ref, ct_ref, st_ref):
    ang = invf_ref[...] * pos_ref[...].astype(F32)
    cos, sin = jnp.cos(ang), jnp.sin(ang)
    ct_ref[...] = cos
    st_ref[...] = sin
    tm = ang.shape[1]
    one = jnp.ones((HEAD_DIM - ROPE_DIM, tm), F32)
    zero = jnp.zeros((ROPE_HALF, tm), F32)
    zero_rest = jnp.zeros((HEAD_DIM - ROPE_DIM, tm), F32)
    c_ref[...] = jnp.concatenate([cos, cos, one] * 2, axis=0).T
    s1_ref[...] = jnp.concatenate([-sin, zero, zero_rest] * 2, axis=0).T
    s2_ref[...] = jnp.concatenate([zero, sin, zero_rest] * 2, axis=0).T


def _rope_tables(positions):
    t = positions.size
    tm = 1024
    inv_freq = ROPE_THETA ** (-jnp.arange(0, ROPE_DIM, 2, dtype=F32) / ROPE_DIM)
    tab = jax.ShapeDtypeStruct((t, LANES), F32)
    tab_t = jax.ShapeDtypeStruct((ROPE_HALF, t), F32)
    row = lambda i: (i, 0)
    col = lambda i: (0, i)
    return pl.pallas_call(
        _rope_table_kernel,
        grid=(t // tm,),
        in_specs=[pl.BlockSpec((1, tm), col), pl.BlockSpec((ROPE_HALF, 1), lambda i: (0, 0))],
        out_specs=[pl.BlockSpec((tm, LANES), row)] * 3 + [pl.BlockSpec((ROPE_HALF, tm), col)] * 2,
        out_shape=[tab] * 3 + [tab_t] * 2,
        compiler_params=_cparams(("parallel",)),
        name="rope_tables",
    )(positions.reshape(1, t), inv_freq.reshape(ROPE_HALF, 1))


def _norm_proj_kernel(x_ref, g_ref, wk_ref, wt_ref, c_ref, s1_ref, s2_ref, ct_ref, st_ref,
                      k_ref, qvt_ref, acc_ref, *, q_tiles):
    x = x_ref[...]
    ms = jnp.mean(x * x, axis=-1, keepdims=True)
    xn = (x * lax.rsqrt(ms + NORM_EPS) * g_ref[...]).astype(BF16)

    c, s1, s2 = c_ref[...], s1_ref[...], s2_ref[...]
    chunk = 512
    for n in range(wk_ref.shape[1] // chunk):
        acc = jnp.dot(xn, wk_ref[:, n * chunk:(n + 1) * chunk], preferred_element_type=F32)
        for t in range(chunk // LANES):
            seg = acc[:, t * LANES:(t + 1) * LANES]
            seg = seg * c + pltpu.roll(seg, LANES - ROPE_HALF, 1) * s1 + pltpu.roll(seg, ROPE_HALF, 1) * s2
            k_ref[n * (chunk // LANES) + t] = seg.astype(BF16)

    acc_ref[...] = lax.dot_general(wt_ref[...], xn, _NT, preferred_element_type=F32)
    ct, st = ct_ref[...], st_ref[...]
    for r in range(qvt_ref.shape[0] // LANES):
        if r not in q_tiles:
            qvt_ref[r * LANES:(r + 1) * LANES, :] = acc_ref[r * LANES:(r + 1) * LANES, :].astype(BF16)
            continue
        for r0 in range(r * LANES, (r + 1) * LANES, HEAD_DIM):
            x1 = acc_ref[r0:r0 + ROPE_HALF, :]
            x2 = acc_ref[r0 + ROPE_HALF:r0 + ROPE_DIM, :]
            rot = jnp.concatenate([x1 * ct - x2 * st, x2 * ct + x1 * st], axis=0)
            qvt_ref[r0:r0 + ROPE_DIM, :] = rot.astype(BF16)
            qvt_ref[r0 + ROPE_DIM:r0 + HEAD_DIM, :] = acc_ref[r0 + ROPE_DIM:r0 + HEAD_DIM, :].astype(BF16)


def _norm_proj(x, g, wk, wt, tabs, q_tiles):
    t, d = x.shape
    nk, nt = wk.shape[1], wt.shape[0]
    tm = PROJ_ROW_TILE
    row = lambda i: (i, 0)
    col = lambda i: (0, i)
    fixed = lambda i: (0, 0)
    once = dict(pipeline_mode=pl.Buffered(1))
    return pl.pallas_call(
        functools.partial(_norm_proj_kernel, q_tiles=frozenset(q_tiles)),
        grid=(t // tm,),
        in_specs=[pl.BlockSpec((tm, d), row), pl.BlockSpec((1, d), fixed),
                  pl.BlockSpec((d, nk), fixed, **once), pl.BlockSpec((nt, d), fixed, **once),
                  pl.BlockSpec((tm, LANES), row), pl.BlockSpec((tm, LANES), row), pl.BlockSpec((tm, LANES), row),
                  pl.BlockSpec((ROPE_HALF, tm), col), pl.BlockSpec((ROPE_HALF, tm), col)],
        out_specs=[pl.BlockSpec((nk // LANES, tm, LANES), lambda i: (0, i, 0)), pl.BlockSpec((nt, tm), col)],
        out_shape=[jax.ShapeDtypeStruct((nk // LANES, t, LANES), BF16), jax.ShapeDtypeStruct((nt, t), BF16)],
        scratch_shapes=[pltpu.VMEM((nt, tm), F32)],
        compiler_params=_cparams(("parallel",)),
        name="norm_proj_rope",
    )(x, g.reshape(1, d), wk, wt, *tabs)


def _row_masks(shape):
    row = lax.broadcasted_iota(jnp.int32, shape, 0)
    return row < HEAD_DIM, row >= HEAD_DIM


def _causal_table_t(t):
    d = np.arange(t)[None, :] - np.arange(t)[:, None]
    return jnp.asarray(np.where(d >= 0, 0.0, NEG), F32)


def _logits_stage(qt, k_of, bias_of, n_blocks, s_ref):
    t = ATT_TILE
    m = None
    for j in range(n_blocks):
        s = jnp.dot(k_of(j), qt, preferred_element_type=F32)
        bias = bias_of(j)
        if bias is not None:
            s = s + bias
        s_ref[j * t:(j + 1) * t, :] = s
        mj = jnp.max(s, axis=0, keepdims=True)
        m = mj if m is None else jnp.maximum(m, mj)
    return m


def _probs_stage(m, n_blocks, s_ref, p_ref, want_sum):
    t = ATT_TILE
    l = None
    for j in range(n_blocks):
        p = jnp.exp2(s_ref[j * t:(j + 1) * t, :] - m)
        p_ref[j * t:(j + 1) * t, :] = p.astype(BF16)
        if want_sum:
            lj = jnp.sum(p, axis=0, keepdims=True)
            l = lj if l is None else l + lj
    return l


def _values_stage(l, vt_of, n_blocks, p_ref, ones_row):
    n = n_blocks * ATT_TILE
    acc = jnp.dot(vt_of(n), p_ref[:n, :], preferred_element_type=F32)
    if l is None:
        l = acc[ones_row:ones_row + 1, :]
    return acc * (1.0 / l)


def _pipelined_attention(n_tiles, make_query, k_of_head, bias_of_tile, vt_of_head, ones_rows, finish_tile,
                         s_refs, p_refs):
    items = [(i, h) for i in reversed(range(n_tiles)) for h in range(2)]
    n = len(items)
    ms, ls, outs = {}, {}, {}
    for step in range(n + 2):
        if 1 <= step <= n:
            c = step - 1
            i, h = items[c]
            ls[c] = _probs_stage(ms.pop(c), i + 1, s_refs.at[c % 2], p_refs.at[c % 2], ones_rows[h] is None)
        if step < n:
            i, h = items[step]
            ms[step] = _logits_stage(make_query(i, h), k_of_head(h), bias_of_tile(i), i + 1, s_refs.at[step % 2])
        if 2 <= step <= n + 1:
            c = step - 2
            i, h = items[c]
            outs[c] = _values_stage(ls.pop(c), vt_of_head(h), i + 1, p_refs.at[c % 2], ones_rows[h])
            if h == 1:
                finish_tile(i, outs.pop(c - 1), outs.pop(c))


def _values_with_ones(vt_ref, vta_ref, vtb_ref):
    vt = vt_ref[...].astype(F32)
    lo, hi = _row_masks(vt.shape)
    vta_ref[...] = jnp.where(lo, vt, 1.0).astype(BF16)
    vtb_ref[...] = jnp.where(hi, vt, 1.0).astype(BF16)
    vt_of = (lambda n: vta_ref[:, :n], lambda n: vtb_ref[:, :n])
    return (lambda h: vt_of[h]), (HEAD_DIM, 0)


def _key_blocks(ref):
    t = ATT_TILE
    if len(ref.shape) == 2:
        return lambda j: ref[j * t:(j + 1) * t, :]
    return lambda j: ref[0, j * t:(j + 1) * t, :]


def _attn_scratch(s):
    return [pltpu.VMEM((2, s, ATT_TILE), F32), pltpu.VMEM((2, s, ATT_TILE), BF16)]


def _k_spec(s, tile0):
    return pl.BlockSpec((1, s, LANES), lambda b, p: (tile0 + p, b, 0))


def _t_spec(s, tile0):
    return pl.BlockSpec((LANES, s), lambda b, p: (tile0 + p, b))


def _dilated_bias_tables_t(t):
    windows = sorted(w for w, _ in DILATED_CONFIGS)
    far = windows[-2] // t + 1
    tabs = []
    for delta in range(far + 1):
        d = delta * t + np.arange(t)[None, :] - np.arange(t)[:, None]
        mult = np.zeros((t, t), np.float64)
        for window, dil in DILATED_CONFIGS:
            mult += (d >= 0) & (d % dil == 0) & (d <= window)
        tabs.append(np.where(mult > 0, np.log2(np.maximum(mult, 1.0)), NEG))
    return jnp.asarray(np.stack(tabs), F32)


def _dilated_kernel(qt_ref, k_ref, vt_ref, bias_ref, o_ref, vta_ref, vtb_ref, s_refs, p_refs):
    t = ATT_TILE
    n_tab = bias_ref.shape[0]
    k_of = _key_blocks(k_ref)
    vt_of_head, ones_rows = _values_with_ones(vt_ref, vta_ref, vtb_ref)

    def make_query(i, h):
        qt = qt_ref[:, i * t:(i + 1) * t].astype(F32)
        return jnp.where(_row_masks(qt.shape)[h], qt, 0.0).astype(BF16)

    def finish_tile(i, out_lo, out_hi):
        lo, _ = _row_masks(out_lo.shape)
        o_ref[0, i * t:(i + 1) * t, :] = jnp.where(lo, out_lo, out_hi).T.astype(BF16)

    _pipelined_attention(qt_ref.shape[1] // t, make_query, lambda h: k_of,
                         lambda i: (lambda j: bias_ref[min(i - j, n_tab - 1)]),
                         vt_of_head, ones_rows, finish_tile, s_refs, p_refs)


def _dilated_attention(kn, qvt, b, s, q_tile0, k_tile0, v_tile0, n_pairs):
    bias = _dilated_bias_tables_t(ATT_TILE)
    return pl.pallas_call(
        _dilated_kernel,
        grid=(b, n_pairs),
        in_specs=[_t_spec(s, q_tile0), _k_spec(s, k_tile0), _t_spec(s, v_tile0),
                  pl.BlockSpec(bias.shape, lambda b_, p: (0, 0, 0))],
        out_specs=_k_spec(s, 0),
        out_shape=jax.ShapeDtypeStruct((n_pairs, b * s, LANES), BF16),
        scratch_shapes=[pltpu.VMEM((LANES, s), BF16), pltpu.VMEM((LANES, s), BF16)] + _attn_scratch(s),
        compiler_params=_cparams(("parallel", "parallel")),
        name="dilated_attention",
    )(qvt, kn, qvt, bias)


def _moba_kernel(qt_ref, k_ref, vt_ref, causal_ref, o_ref, ka_ref, kb_ref, vta_ref, vtb_ref, s_refs, p_refs):
    t = MOBA_BLOCK
    vt_of_head, ones_rows = _values_with_ones(vt_ref, vta_ref, vtb_ref)
    s_len = k_ref.shape[1]
    nblk = s_len // t

    k = k_ref[0].astype(F32)
    lane_k = lax.broadcasted_iota(jnp.int32, k.shape, 1)
    blk = lax.broadcasted_iota(jnp.int32, k.shape, 0) // t
    ka_ref[...] = jnp.where(lane_k < HEAD_DIM, k, jnp.where(lane_k - HEAD_DIM == blk, 1.0, 0.0)).astype(BF16)
    kb_ref[...] = jnp.where(lane_k >= HEAD_DIM, k, jnp.where(lane_k == blk, 1.0, 0.0)).astype(BF16)
    means = jnp.concatenate(
        [jnp.mean(k_ref[0, n * t:(n + 1) * t, :].astype(F32), axis=0, keepdims=True) for n in range(nblk)], axis=0)
    lane_m = lax.broadcasted_iota(jnp.int32, means.shape, 1)
    means2 = jnp.concatenate([jnp.where(lane_m < HEAD_DIM, means, 0.0),
                              jnp.where(lane_m >= HEAD_DIM, means, 0.0)], axis=0)

    zeros = jnp.zeros((HEAD_DIM - nblk, t), F32)
    k_of = (_key_blocks(ka_ref), _key_blocks(kb_ref))

    def make_query(i, h):
        qt = qt_ref[:, i * t:(i + 1) * t].astype(F32)
        if i <= MOBA_TOPK:
            return jnp.where(_row_masks(qt.shape)[h], qt, 0.0).astype(BF16)
        g = jnp.dot(means2[h * nblk:(h + 1) * nblk], qt, preferred_element_type=F32,
                    precision=lax.Precision.HIGHEST)
        row = lax.broadcasted_iota(jnp.int32, g.shape, 0)
        rank = jnp.zeros(g.shape, F32)
        for kk in range(i):
            gk = g[kk:kk + 1, :]
            beats = (gk > g) | ((gk == g) & (row > kk))
            rank = rank + jnp.where(beats, 1.0, 0.0)
        drop = jnp.where((row < i) & (rank >= MOBA_TOPK), NEG, 0.0)
        parts = [qt[:HEAD_DIM], drop, zeros] if h == 0 else [drop, zeros, qt[HEAD_DIM:]]
        return jnp.concatenate(parts, axis=0).astype(BF16)

    def finish_tile(i, out_lo, out_hi):
        lo, _ = _row_masks(out_lo.shape)
        o_ref[0, i * t:(i + 1) * t, :] = jnp.where(lo, out_lo, out_hi).T.astype(BF16)

    _pipelined_attention(nblk, make_query, lambda h: k_of[h],
                         lambda i: (lambda j: causal_ref[...] if j == i else None),
                         vt_of_head, ones_rows, finish_tile, s_refs, p_refs)


def _moba_attention(kn, qvt, b, s, q_tile0, k_tile0, v_tile0, n_pairs):
    t = MOBA_BLOCK
    assert t == ATT_TILE and s // t <= SUBLANES
    return pl.pallas_call(
        _moba_kernel,
        grid=(b, n_pairs),
        in_specs=[_t_spec(s, q_tile0), _k_spec(s, k_tile0), _t_spec(s, v_tile0),
                  pl.BlockSpec((t, t), lambda b_, p: (0, 0))],
        out_specs=_k_spec(s, 0),
        out_shape=jax.ShapeDtypeStruct((n_pairs, b * s, LANES), BF16),
        scratch_shapes=[pltpu.VMEM((s, LANES), BF16), pltpu.VMEM((s, LANES), BF16),
                        pltpu.VMEM((LANES, s), BF16), pltpu.VMEM((LANES, s), BF16)] + _attn_scratch(s),
        compiler_params=_cparams(("parallel", "parallel")),
        name="moba_attention",
    )(qvt, kn, qvt, _causal_table_t(t))


def _diff_kernel(qt_ref, k_ref, vt_ref, causal_ref, lq1_ref, lk1_ref, lq2_ref, lk2_ref, g_ref, o_ref,
                 s_refs, p_refs, *, lambda_init):
    t = ATT_TILE
    k_of = _key_blocks(k_ref)
    lam = (jnp.exp(jnp.sum(lq1_ref[...] * lk1_ref[...], axis=-1, keepdims=True))
           - jnp.exp(jnp.sum(lq2_ref[...] * lk2_ref[...], axis=-1, keepdims=True)) + lambda_init)

    def make_query(i, h):
        qt = qt_ref[:, i * t:(i + 1) * t].astype(F32)
        return jnp.where(_row_masks(qt.shape)[h], qt, 0.0).astype(BF16)

    def finish_tile(i, out1, out2):
        o = out1 - lam * out2
        y = o * lax.rsqrt(jnp.mean(o * o, axis=0, keepdims=True) + NORM_EPS)
        o_ref[0, i * t:(i + 1) * t, :] = ((y.T * g_ref[...]) * (1.0 - lambda_init)).astype(BF16)

    _pipelined_attention(qt_ref.shape[1] // t, make_query, lambda h: k_of,
                         lambda i: (lambda j: causal_ref[...] if j == i else None),
                         lambda h: (lambda n: vt_ref[:, :n]), (None, None), finish_tile, s_refs, p_refs)


def _diff_attention(kn, qvt, b, s, lq1, lk1, lq2, lk2, subln_g, lambda_init, n_heads):
    t = ATT_TILE
    vec = lambda a: a.reshape(1, -1).astype(F32)
    small = lambda n: pl.BlockSpec((1, n), lambda b_, h: (0, 0))
    return pl.pallas_call(
        functools.partial(_diff_kernel, lambda_init=lambda_init),
        grid=(b, n_heads),
        in_specs=[_t_spec(s, 0), _k_spec(s, 0), _t_spec(s, n_heads), pl.BlockSpec((t, t), lambda b_, h: (0, 0)),
                  small(HEAD_DIM), small(HEAD_DIM), small(HEAD_DIM), small(HEAD_DIM), small(LANES)],
        out_specs=_k_spec(s, 0),
        out_shape=jax.ShapeDtypeStruct((n_heads, b * s, LANES), BF16),
        scratch_shapes=_attn_scratch(s),
        compiler_params=_cparams(("parallel", "parallel")),
        name="diff_attention",
    )(qvt, kn, qvt, _causal_table_t(t), vec(lq1), vec(lk1), vec(lq2), vec(lk2), vec(subln_g))


def _mixer_out_ffn_kernel(*refs, n_acts, chunk, final_norm):
    h_ref = refs[0]
    act_refs = refs[1:1 + n_acts]
    wo_ref, g_ref, wg_ref, wu_ref, wd_ref, gf_ref, o_ref = refs[1 + n_acts:]
    act = jnp.concatenate([a_ref[c] for a_ref in act_refs for c in range(a_ref.shape[0])], axis=1)
    x = h_ref[...] + jnp.dot(act, wo_ref[...], preferred_element_type=F32)
    ms = jnp.mean(x * x, axis=-1, keepdims=True)
    xn = (x * lax.rsqrt(ms + NORM_EPS) * g_ref[...]).astype(BF16)
    acc = x
    for c in range(wg_ref.shape[1] // chunk):
        sl = slice(c * chunk, (c + 1) * chunk)
        gate = jnp.dot(xn, wg_ref[:, sl], preferred_element_type=F32)
        up = jnp.dot(xn, wu_ref[:, sl], preferred_element_type=F32)
        mid = (gate * jax.nn.sigmoid(gate) * up).astype(BF16)
        acc = acc + jnp.dot(mid, wd_ref[sl, :], preferred_element_type=F32)
    if final_norm:
        ms = jnp.mean(acc * acc, axis=-1, keepdims=True)
        acc = acc * lax.rsqrt(ms + NORM_EPS) * gf_ref[...]
    o_ref[...] = acc


def _mixer_out_ffn(h, acts, w_out, g, layer, wg, wu, wd, g_final, final_norm):
    t, d = h.shape
    ff = wg.shape[2]
    tm = ROW_TILE
    row = lambda i: (i, 0)
    fixed = lambda i: (0, 0)
    of_layer = lambda i: (layer, 0, 0)
    once = dict(pipeline_mode=pl.Buffered(1))
    in_specs = [pl.BlockSpec((tm, d), row)]
    in_specs += [pl.BlockSpec((a.shape[0], tm, LANES), lambda i: (0, i, 0)) for a in acts]
    in_specs += [pl.BlockSpec(w_out.shape, fixed, **once), pl.BlockSpec((1, d), fixed),
                 pl.BlockSpec((None, d, ff), of_layer, **once), pl.BlockSpec((None, d, ff), of_layer, **once),
                 pl.BlockSpec((None, ff, d), of_layer, **once), pl.BlockSpec((1, d), fixed)]
    args = [h, *acts, w_out, g.reshape(1, d), wg, wu, wd, g_final.reshape(1, d)]
    return pl.pallas_call(
        functools.partial(_mixer_out_ffn_kernel, n_acts=len(acts), chunk=256, final_norm=final_norm),
        grid=(t // tm,),
        in_specs=in_specs,
        out_specs=pl.BlockSpec((tm, d), row),
        out_shape=jax.ShapeDtypeStruct((t, d), F32),
        compiler_params=_cparams(("parallel",)),
        name="mixer_out_ffn",
    )(*args)


def _cast_kernel(*refs):
    n = len(refs) // 2
    for src_ref, dst_ref in zip(refs[:n], refs[n:]):
        dst_ref[...] = src_ref[...].astype(BF16)


def _cast_bf16(arrays, n_chunks):
    specs = [pl.BlockSpec((1, a.shape[1] // n_chunks, a.shape[2]), lambda l, c: (l, c, 0)) for a in arrays]
    return pl.pallas_call(
        _cast_kernel,
        grid=(arrays[0].shape[0], n_chunks),
        in_specs=specs,
        out_specs=specs,
        out_shape=[jax.ShapeDtypeStruct(a.shape, BF16) for a in arrays],
        compiler_params=_cparams(("parallel", "parallel")),
        name="cast_weights",
    )(*arrays)


def _lambda_init(layer_idx):
    return 0.8 - 0.6 * math.exp(-0.3 * layer_idx)


def _split_weights(w_in, q_cols, k_cols, v_cols):
    cat = lambda cols: jnp.concatenate([w_in[:, a:b] for a, b in cols], axis=1)
    groups = [(cat([c]) * (ATTN_SCALE * LOG2E) if is_q else cat([c])) for c, is_q in
              sorted([(c, True) for c in q_cols] + [(c, False) for c in v_cols])]
    return cat(k_cols).astype(BF16), jnp.concatenate(groups, axis=1).T.astype(BF16)


def kernel(x, positions, ab_norm_g, ab_w_in, ab_w_out, diff_norm_g, diff_w_in, diff_w_out, diff_lambda_q1, diff_lambda_k1, diff_lambda_q2, diff_lambda_k2, diff_subln_g, ffn_norm_g, ffn_w_gate, ffn_w_up, ffn_w_down, final_norm_g):
    b, s, d = x.shape
    t = b * s
    n_pairs = d // (4 * HEAD_DIM)
    n_diff = d // (2 * HEAD_DIM)
    wa = n_pairs * LANES
    tabs = _rope_tables(positions)
    wg, wu, wd = _cast_bf16([ffn_w_gate, ffn_w_up, ffn_w_down], n_chunks=4)
    h = x.reshape(t, d)

    wk, wt = _split_weights(ab_w_in[0], q_cols=[(0, wa), (3 * wa, 4 * wa)],
                            k_cols=[(wa, 2 * wa), (4 * wa, 5 * wa)], v_cols=[(2 * wa, 3 * wa), (5 * wa, 6 * wa)])
    q_tiles0 = list(range(n_pairs)) + list(range(2 * n_pairs, 3 * n_pairs))
    kn, qvt = _norm_proj(h, ab_norm_g[0], wk, wt, tabs, q_tiles0)
    oa = _dilated_attention(kn, qvt, b, s, 0, 0, n_pairs, n_pairs)
    ob = _moba_attention(kn, qvt, b, s, 2 * n_pairs, n_pairs, 3 * n_pairs, n_pairs)
    h = _mixer_out_ffn(h, [oa, ob], ab_w_out[0].astype(BF16), ffn_norm_g[0], 0, wg, wu, wd, final_norm_g, False)

    wk, wt = _split_weights(diff_w_in[0], q_cols=[(0, d)], k_cols=[(d, 2 * d)], v_cols=[(2 * d, 3 * d)])
    kn, qvt = _norm_proj(h, diff_norm_g[0], wk, wt, tabs, list(range(n_diff)))
    od = _diff_attention(kn, qvt, b, s, diff_lambda_q1[0], diff_lambda_k1[0],
                         diff_lambda_q2[0], diff_lambda_k2[0], diff_subln_g[0], _lambda_init(1), n_diff)
    h = _mixer_out_ffn(h, [od], diff_w_out[0].astype(BF16), ffn_norm_g[1], 1, wg, wu, wd, final_norm_g, True)
    return h.reshape(b, s, d)
```

```python
import functools
import math

import numpy as np
import jax
import jax.numpy as jnp
from jax import lax
from jax.experimental import pallas as pl
from jax.experimental.pallas import tpu as pltpu

D_MODEL = 1024
HEAD_DIM = 64
LANES = 128
SUBLANES = 8
ROPE_DIM = HEAD_DIM // 4
ROPE_HALF = ROPE_DIM // 2
ROPE_THETA = 500000.0
D_FF = 2816
NORM_EPS = 1e-5
ATTN_SCALE = HEAD_DIM ** -0.5
LOG2E = math.log2(math.e)
NEG = -1e30
DILATED_CONFIGS = ((128, 1), (512, 4), (2048, 16))
MOBA_BLOCK = 256
MOBA_TOPK = 3

ATT_TILE = 256
ROW_TILE = 512
PROJ_ROW_TILE = 1024
VMEM_LIMIT = 56 * 1024 * 1024

F32 = jnp.float32
BF16 = jnp.bfloat16
_NT = (((1,), (1,)), ((), ()))

assert ROPE_HALF == SUBLANES


def _cparams(sem):
    return pltpu.CompilerParams(dimension_semantics=sem, vmem_limit_bytes=VMEM_LIMIT)


def _rope_table_kernel(pos_ref, invf_ref, c_ref, s1_ref, s2_ref, ct_ref, st_ref):
    ang = invf_ref[...] * pos_ref[...].astype(F32)
    cos, sin = jnp.cos(ang), jnp.sin(ang)
    ct_ref[...] = cos
    st_ref[...] = sin
    tm = ang.shape[1]
    one = jnp.ones((HEAD_DIM - ROPE_DIM, tm), F32)
    zero = jnp.zeros((ROPE_HALF, tm), F32)
    zero_rest = jnp.zeros((HEAD_DIM - ROPE_DIM, tm), F32)
    c_ref[...] = jnp.concatenate([cos, cos, one] * 2, axis=0).T
    s1_ref[...] = jnp.concatenate([-sin, zero, zero_rest] * 2, axis=0).T
    s2_ref[...] = jnp.concatenate([zero, sin, zero_rest] * 2, axis=0).T


def _rope_tables(positions):
    t = positions.size
    tm = 1024
    inv_freq = ROPE_THETA ** (-jnp.arange(0, ROPE_DIM, 2, dtype=F32) / ROPE_DIM)
    tab = jax.ShapeDtypeStruct((t, LANES), F32)
    tab_t = jax.ShapeDtypeStruct((ROPE_HALF, t), F32)
    row = lambda i: (i, 0)
    col = lambda i: (0, i)
    return pl.pallas_call(
        _rope_table_kernel,
        grid=(t // tm,),
        in_specs=[pl.BlockSpec((1, tm), col), pl.BlockSpec((ROPE_HALF, 1), lambda i: (0, 0))],
        out_specs=[pl.BlockSpec((tm, LANES), row)] * 3 + [pl.BlockSpec((ROPE_HALF, tm), col)] * 2,
        out_shape=[tab] * 3 + [tab_t] * 2,
        compiler_params=_cparams(("parallel",)),
        name="rope_tables",
    )(positions.reshape(1, t), inv_freq.reshape(ROPE_HALF, 1))


def _norm_proj_kernel(x_ref, g_ref, wk_ref, wt_ref, c_ref, s1_ref, s2_ref, ct_ref, st_ref,
                      k_ref, qvt_ref, acc_ref, *, q_tiles):
    x = x_ref[...]
    ms = jnp.mean(x * x, axis=-1, keepdims=True)
    xn = (x * lax.rsqrt(ms + NORM_EPS) * g_ref[...]).astype(BF16)

    c, s1, s2 = c_ref[...], s1_ref[...], s2_ref[...]
    chunk = 512
    for n in range(wk_ref.shape[1] // chunk):
        acc = jnp.dot(xn, wk_ref[:, n * chunk:(n + 1) * chunk], preferred_element_type=F32)
        for t in range(chunk // LANES):
            seg = acc[:, t * LANES:(t + 1) * LANES]
            seg = seg * c + pltpu.roll(seg, LANES - ROPE_HALF, 1) * s1 + pltpu.roll(seg, ROPE_HALF, 1) * s2
            k_ref[n * (chunk // LANES) + t] = seg.astype(BF16)

    acc_ref[...] = lax.dot_general(wt_ref[...], xn, _NT, preferred_element_type=F32)
    ct, st = ct_ref[...], st_ref[...]
    for r in range(qvt_ref.shape[0] // LANES):
        if r not in q_tiles:
            qvt_ref[r * LANES:(r + 1) * LANES, :] = acc_ref[r * LANES:(r + 1) * LANES, :].astype(BF16)
            continue
        for r0 in range(r * LANES, (r + 1) * LANES, HEAD_DIM):
            x1 = acc_ref[r0:r0 + ROPE_HALF, :]
            x2 = acc_ref[r0 + ROPE_HALF:r0 + ROPE_DIM, :]
            rot = jnp.concatenate([x1 * ct - x2 * st, x2 * ct + x1 * st], axis=0)
            qvt_ref[r0:r0 + ROPE_DIM, :] = rot.astype(BF16)
            qvt_ref[r0 + ROPE_DIM:r0 + HEAD_DIM, :] = acc_ref[r0 + ROPE_DIM:r0 + HEAD_DIM, :].astype(BF16)


def _norm_proj(x, g, wk, wt, tabs, q_tiles):
    t, d = x.shape
    nk, nt = wk.shape[1], wt.shape[0]
    tm = PROJ_ROW_TILE
    row = lambda i: (i, 0)
    col = lambda i: (0, i)
    fixed = lambda i: (0, 0)
    once = dict(pipeline_mode=pl.Buffered(1))
    return pl.pallas_call(
        functools.partial(_norm_proj_kernel, q_tiles=frozenset(q_tiles)),
        grid=(t // tm,),
        in_specs=[pl.BlockSpec((tm, d), row), pl.BlockSpec((1, d), fixed),
                  pl.BlockSpec((d, nk), fixed, **once), pl.BlockSpec((nt, d), fixed, **once),
                  pl.BlockSpec((tm, LANES), row), pl.BlockSpec((tm, LANES), row), pl.BlockSpec((tm, LANES), row),
                  pl.BlockSpec((ROPE_HALF, tm), col), pl.BlockSpec((ROPE_HALF, tm), col)],
        out_specs=[pl.BlockSpec((nk // LANES, tm, LANES), lambda i: (0, i, 0)), pl.BlockSpec((nt, tm), col)],
        out_shape=[jax.ShapeDtypeStruct((nk // LANES, t, LANES), BF16), jax.ShapeDtypeStruct((nt, t), BF16)],
        scratch_shapes=[pltpu.VMEM((nt, tm), F32)],
        compiler_params=_cparams(("parallel",)),
        name="norm_proj_rope",
    )(x, g.reshape(1, d), wk, wt, *tabs)


def _row_masks(shape):
    row = lax.broadcasted_iota(jnp.int32, shape, 0)
    return row < HEAD_DIM, row >= HEAD_DIM


def _causal_table_t(t):
    d = np.arange(t)[None, :] - np.arange(t)[:, None]
    return jnp.asarray(np.where(d >= 0, 0.0, NEG), F32)


def _logits_stage(qt, k_of, bias_of, n_blocks, s_ref):
    t = ATT_TILE
    m = None
    s_all = jnp.dot(k_of(n_blocks), qt, preferred_element_type=F32)
    for j in range(n_blocks):
        s = s_all[j * t:(j + 1) * t]
        bias = bias_of(j)
        if bias is not None:
            s = s + bias
        s_ref[j * t:(j + 1) * t, :] = s
        mj = jnp.max(s, axis=0, keepdims=True)
        m = mj if m is None else jnp.maximum(m, mj)
    return m


def _probs_stage(m, n_blocks, s_ref, p_ref, want_sum):
    t = ATT_TILE
    l = None
    for j in range(n_blocks):
        p = jnp.exp2(s_ref[j * t:(j + 1) * t, :] - m)
        p_ref[j * t:(j + 1) * t, :] = p.astype(BF16)
        if want_sum:
            lj = jnp.sum(p, axis=0, keepdims=True)
            l = lj if l is None else l + lj
    return l


def _values_stage(l, vt_of, n_blocks, p_ref, ones_row):
    n = n_blocks * ATT_TILE
    acc = jnp.dot(vt_of(n), p_ref[:n, :], preferred_element_type=F32)
    if l is None:
        l = acc[ones_row:ones_row + 1, :]
    return acc * (1.0 / l)


def _pipelined_attention(n_tiles, make_query, k_of_head, bias_of_tile, vt_of_head, ones_rows, finish_tile,
                         s_refs, p_refs):
    items = [(i, h) for i in reversed(range(n_tiles)) for h in range(2)]
    n = len(items)
    ms, ls, outs = {}, {}, {}
    for step in range(n + 2):
        if 1 <= step <= n:
            c = step - 1
            i, h = items[c]
            ls[c] = _probs_stage(ms.pop(c), i + 1, s_refs.at[c % 2], p_refs.at[c % 2], ones_rows[h] is None)
        if step < n:
            i, h = items[step]
            ms[step] = _logits_stage(make_query(i, h), k_of_head(h), bias_of_tile(i), i + 1, s_refs.at[step % 2])
        if 2 <= step <= n + 1:
            c = step - 2
            i, h = items[c]
            outs[c] = _values_stage(ls.pop(c), vt_of_head(h), i + 1, p_refs.at[c % 2], ones_rows[h])
            if h == 1:
                finish_tile(i, outs.pop(c - 1), outs.pop(c))


def _values_with_ones(vt_ref, vta_ref, vtb_ref):
    vt = vt_ref[...].astype(F32)
    lo, hi = _row_masks(vt.shape)
    vta_ref[...] = jnp.where(lo, vt, 1.0).astype(BF16)
    vtb_ref[...] = jnp.where(hi, vt, 1.0).astype(BF16)
    vt_of = (lambda n: vta_ref[:, :n], lambda n: vtb_ref[:, :n])
    return (lambda h: vt_of[h]), (HEAD_DIM, 0)


def _key_blocks(ref):
    t = ATT_TILE
    if len(ref.shape) == 2:
        return lambda n: ref[:n * t, :]
    return lambda n: ref[0, :n * t, :]


def _attn_scratch(s):
    return [pltpu.VMEM((2, s, ATT_TILE), F32), pltpu.VMEM((2, s, ATT_TILE), BF16)]


def _k_spec(s, tile0):
    return pl.BlockSpec((1, s, LANES), lambda b, p: (tile0 + p, b, 0))


def _t_spec(s, tile0):
    return pl.BlockSpec((LANES, s), lambda b, p: (tile0 + p, b))


def _dilated_bias_tables_t(t):
    windows = sorted(w for w, _ in DILATED_CONFIGS)
    far = windows[-2] // t + 1
    tabs = []
    for delta in range(far + 1):
        d = delta * t + np.arange(t)[None, :] - np.arange(t)[:, None]
        mult = np.zeros((t, t), np.float64)
        for window, dil in DILATED_CONFIGS:
            mult += (d >= 0) & (d % dil == 0) & (d <= window)
        tabs.append(np.where(mult > 0, np.log2(np.maximum(mult, 1.0)), NEG))
    return jnp.asarray(np.stack(tabs), F32)


def _dilated_kernel(qt_ref, k_ref, vt_ref, bias_ref, o_ref, vta_ref, vtb_ref, s_refs, p_refs):
    t = ATT_TILE
    n_tab = bias_ref.shape[0]
    k_of = _key_blocks(k_ref)
    vt_of_head, ones_rows = _values_with_ones(vt_ref, vta_ref, vtb_ref)

    def make_query(i, h):
        qt = qt_ref[:, i * t:(i + 1) * t].astype(F32)
        return jnp.where(_row_masks(qt.shape)[h], qt, 0.0).astype(BF16)

    def finish_tile(i, out_lo, out_hi):
        lo, _ = _row_masks(out_lo.shape)
        o_ref[0, i * t:(i + 1) * t, :] = jnp.where(lo, out_lo, out_hi).T.astype(BF16)

    _pipelined_attention(qt_ref.shape[1] // t, make_query, lambda h: k_of,
                         lambda i: (lambda j: bias_ref[min(i - j, n_tab - 1)]),
                         vt_of_head, ones_rows, finish_tile, s_refs, p_refs)


def _dilated_attention(kn, qvt, b, s, q_tile0, k_tile0, v_tile0, n_pairs):
    bias = _dilated_bias_tables_t(ATT_TILE)
    return pl.pallas_call(
        _dilated_kernel,
        grid=(b, n_pairs),
        in_specs=[_t_spec(s, q_tile0), _k_spec(s, k_tile0), _t_spec(s, v_tile0),
                  pl.BlockSpec(bias.shape, lambda b_, p: (0, 0, 0))],
        out_specs=_k_spec(s, 0),
        out_shape=jax.ShapeDtypeStruct((n_pairs, b * s, LANES), BF16),
        scratch_shapes=[pltpu.VMEM((LANES, s), BF16), pltpu.VMEM((LANES, s), BF16)] + _attn_scratch(s),
        compiler_params=_cparams(("parallel", "parallel")),
        name="dilated_attention",
    )(qvt, kn, qvt, bias)


def _moba_kernel(qt_ref, k_ref, vt_ref, causal_ref, o_ref, ka_ref, kb_ref, vta_ref, vtb_ref, s_refs, p_refs):
    t = MOBA_BLOCK
    vt_of_head, ones_rows = _values_with_ones(vt_ref, vta_ref, vtb_ref)
    s_len = k_ref.shape[1]
    nblk = s_len // t

    k = k_ref[0].astype(F32)
    lane_k = lax.broadcasted_iota(jnp.int32, k.shape, 1)
    blk = lax.broadcasted_iota(jnp.int32, k.shape, 0) // t
    ka_ref[...] = jnp.where(lane_k < HEAD_DIM, k, jnp.where(lane_k - HEAD_DIM == blk, 1.0, 0.0)).astype(BF16)
    kb_ref[...] = jnp.where(lane_k >= HEAD_DIM, k, jnp.where(lane_k == blk, 1.0, 0.0)).astype(BF16)
    means = jnp.concatenate(
        [jnp.mean(k_ref[0, n * t:(n + 1) * t, :].astype(F32), axis=0, keepdims=True) for n in range(nblk)], axis=0)
    lane_m = lax.broadcasted_iota(jnp.int32, means.shape, 1)
    means2 = jnp.concatenate([jnp.where(lane_m < HEAD_DIM, means, 0.0),
                              jnp.where(lane_m >= HEAD_DIM, means, 0.0)], axis=0)

    zeros = jnp.zeros((HEAD_DIM - nblk, t), F32)
    k_of = (_key_blocks(ka_ref), _key_blocks(kb_ref))

    def make_query(i, h):
        qt = qt_ref[:, i * t:(i + 1) * t].astype(F32)
        if i <= MOBA_TOPK:
            return jnp.where(_row_masks(qt.shape)[h], qt, 0.0).astype(BF16)
        g = jnp.dot(means2[h * nblk:(h + 1) * nblk], qt, preferred_element_type=F32,
                    precision=lax.Precision.HIGHEST)
        row = lax.broadcasted_iota(jnp.int32, g.shape, 0)
        rank = jnp.zeros(g.shape, F32)
        for kk in range(i):
            gk = g[kk:kk + 1, :]
            beats = (gk > g) | ((gk == g) & (row > kk))
            rank = rank + jnp.where(beats, 1.0, 0.0)
        drop = jnp.where((row < i) & (rank >= MOBA_TOPK), NEG, 0.0)
        parts = [qt[:HEAD_DIM], drop, zeros] if h == 0 else [drop, zeros, qt[HEAD_DIM:]]
        return jnp.concatenate(parts, axis=0).astype(BF16)

    def finish_tile(i, out_lo, out_hi):
        lo, _ = _row_masks(out_lo.shape)
        o_ref[0, i * t:(i + 1) * t, :] = jnp.where(lo, out_lo, out_hi).T.astype(BF16)

    _pipelined_attention(nblk, make_query, lambda h: k_of[h],
                         lambda i: (lambda j: causal_ref[...] if j == i else None),
                         vt_of_head, ones_rows, finish_tile, s_refs, p_refs)


def _moba_attention(kn, qvt, b, s, q_tile0, k_tile0, v_tile0, n_pairs):
    t = MOBA_BLOCK
    assert t == ATT_TILE and s // t <= SUBLANES
    return pl.pallas_call(
        _moba_kernel,
        grid=(b, n_pairs),
        in_specs=[_t_spec(s, q_tile0), _k_spec(s, k_tile0), _t_spec(s, v_tile0),
                  pl.BlockSpec((t, t), lambda b_, p: (0, 0))],
        out_specs=_k_spec(s, 0),
        out_shape=jax.ShapeDtypeStruct((n_pairs, b * s, LANES), BF16),
        scratch_shapes=[pltpu.VMEM((s, LANES), BF16), pltpu.VMEM((s, LANES), BF16),
                        pltpu.VMEM((LANES, s), BF16), pltpu.VMEM((LANES, s), BF16)] + _attn_scratch(s),
        compiler_params=_cparams(("parallel", "parallel")),
        name="moba_attention",
    )(qvt, kn, qvt, _causal_table_t(t))


def _diff_kernel(qt_ref, k_ref, vt_ref, causal_ref, lq1_ref, lk1_ref, lq2_ref, lk2_ref, g_ref, o_ref,
                 s_refs, p_refs, *, lambda_init):
    t = ATT_TILE
    k_of = _key_blocks(k_ref)
    lam = (jnp.exp(jnp.sum(lq1_ref[...] * lk1_ref[...], axis=-1, keepdims=True))
           - jnp.exp(jnp.sum(lq2_ref[...] * lk2_ref[...], axis=-1, keepdims=True)) + lambda_init)

    def make_query(i, h):
        qt = qt_ref[:, i * t:(i + 1) * t].astype(F32)
        return jnp.where(_row_masks(qt.shape)[h], qt, 0.0).astype(BF16)

    def finish_tile(i, out1, out2):
        o = out1 - lam * out2
        y = o * lax.rsqrt(jnp.mean(o * o, axis=0, keepdims=True) + NORM_EPS)
        o_ref[0, i * t:(i + 1) * t, :] = ((y.T * g_ref[...]) * (1.0 - lambda_init)).astype(BF16)

    _pipelined_attention(qt_ref.shape[1] // t, make_query, lambda h: k_of,
                         lambda i: (lambda j: causal_ref[...] if j == i else None),
                         lambda h: (lambda n: vt_ref[:, :n]), (None, None), finish_tile, s_refs, p_refs)


def _diff_attention(kn, qvt, b, s, lq1, lk1, lq2, lk2, subln_g, lambda_init, n_heads):
    t = ATT_TILE
    vec = lambda a: a.reshape(1, -1).astype(F32)
    small = lambda n: pl.BlockSpec((1, n), lambda b_, h: (0, 0))
    return pl.pallas_call(
        functools.partial(_diff_kernel, lambda_init=lambda_init),
        grid=(b, n_heads),
        in_specs=[_t_spec(s, 0), _k_spec(s, 0), _t_spec(s, n_heads), pl.BlockSpec((t, t), lambda b_, h: (0, 0)),
                  small(HEAD_DIM), small(HEAD_DIM), small(HEAD_DIM), small(HEAD_DIM), small(LANES)],
        out_specs=_k_spec(s, 0),
        out_shape=jax.ShapeDtypeStruct((n_heads, b * s, LANES), BF16),
        scratch_shapes=_attn_scratch(s),
        compiler_params=_cparams(("parallel", "parallel")),
        name="diff_attention",
    )(qvt, kn, qvt, _causal_table_t(t), vec(lq1), vec(lk1), vec(lq2), vec(lk2), vec(subln_g))


def _mixer_out_ffn_kernel(*refs, n_acts, chunk, final_norm):
    h_ref = refs[0]
    act_refs = refs[1:1 + n_acts]
    wo_ref, g_ref, wg_ref, wu_ref, wd_ref, gf_ref, o_ref = refs[1 + n_acts:]
    act = jnp.concatenate([a_ref[c] for a_ref in act_refs for c in range(a_ref.shape[0])], axis=1)
    x = h_ref[...] + jnp.dot(act, wo_ref[...], preferred_element_type=F32)
    ms = jnp.mean(x * x, axis=-1, keepdims=True)
    xn = (x * lax.rsqrt(ms + NORM_EPS) * g_ref[...]).astype(BF16)
    acc = x
    for c in range(wg_ref.shape[1] // chunk):
        sl = slice(c * chunk, (c + 1) * chunk)
        gate = jnp.dot(xn, wg_ref[:, sl], preferred_element_type=F32)
        up = jnp.dot(xn, wu_ref[:, sl], preferred_element_type=F32)
        mid = (gate * jax.nn.sigmoid(gate) * up).astype(BF16)
        acc = acc + jnp.dot(mid, wd_ref[sl, :], preferred_element_type=F32)
    if final_norm:
        ms = jnp.mean(acc * acc, axis=-1, keepdims=True)
        acc = acc * lax.rsqrt(ms + NORM_EPS) * gf_ref[...]
    o_ref[...] = acc


def _mixer_out_ffn(h, acts, w_out, g, layer, wg, wu, wd, g_final, final_norm):
    t, d = h.shape
    ff = wg.shape[2]
    tm = ROW_TILE
    row = lambda i: (i, 0)
    fixed = lambda i: (0, 0)
    of_layer = lambda i: (layer, 0, 0)
    once = dict(pipeline_mode=pl.Buffered(1))
    in_specs = [pl.BlockSpec((tm, d), row)]
    in_specs += [pl.BlockSpec((a.shape[0], tm, LANES), lambda i: (0, i, 0)) for a in acts]
    in_specs += [pl.BlockSpec(w_out.shape, fixed, **once), pl.BlockSpec((1, d), fixed),
                 pl.BlockSpec((None, d, ff), of_layer, **once), pl.BlockSpec((None, d, ff), of_layer, **once),
                 pl.BlockSpec((None, ff, d), of_layer, **once), pl.BlockSpec((1, d), fixed)]
    args = [h, *acts, w_out, g.reshape(1, d), wg, wu, wd, g_final.reshape(1, d)]
    return pl.pallas_call(
        functools.partial(_mixer_out_ffn_kernel, n_acts=len(acts), chunk=256, final_norm=final_norm),
        grid=(t // tm,),
        in_specs=in_specs,
        out_specs=pl.BlockSpec((tm, d), row),
        out_shape=jax.ShapeDtypeStruct((t, d), F32),
        compiler_params=_cparams(("parallel",)),
        name="mixer_out_ffn",
    )(*args)


def _cast_kernel(*refs):
    n = len(refs) // 2
    for src_ref, dst_ref in zip(refs[:n], refs[n:]):
        dst_ref[...] = src_ref[...].astype(BF16)


def _cast_bf16(arrays, n_chunks):
    specs = [pl.BlockSpec((1, a.shape[1] // n_chunks, a.shape[2]), lambda l, c: (l, c, 0)) for a in arrays]
    return pl.pallas_call(
        _cast_kernel,
        grid=(arrays[0].shape[0], n_chunks),
        in_specs=specs,
        out_specs=specs,
        out_shape=[jax.ShapeDtypeStruct(a.shape, BF16) for a in arrays],
        compiler_params=_cparams(("parallel", "parallel")),
        name="cast_weights",
    )(*arrays)


def _lambda_init(layer_idx):
    return 0.8 - 0.6 * math.exp(-0.3 * layer_idx)


def _split_weights(w_in, q_cols, k_cols, v_cols):
    cat = lambda cols: jnp.concatenate([w_in[:, a:b] for a, b in cols], axis=1)
    groups = [(cat([c]) * (ATTN_SCALE * LOG2E) if is_q else cat([c])) for c, is_q in
              sorted([(c, True) for c in q_cols] + [(c, False) for c in v_cols])]
    return cat(k_cols).astype(BF16), jnp.concatenate(groups, axis=1).T.astype(BF16)


def kernel(x, positions, ab_norm_g, ab_w_in, ab_w_out, diff_norm_g, diff_w_in, diff_w_out, diff_lambda_q1, diff_lambda_k1, diff_lambda_q2, diff_lambda_k2, diff_subln_g, ffn_norm_g, ffn_w_gate, ffn_w_up, ffn_w_down, final_norm_g):
    b, s, d = x.shape
    t = b * s
    n_pairs = d // (4 * HEAD_DIM)
    n_diff = d // (2 * HEAD_DIM)
    wa = n_pairs * LANES
    tabs = _rope_tables(positions)
    wg, wu, wd = _cast_bf16([ffn_w_gate, ffn_w_up, ffn_w_down], n_chunks=4)
    h = x.reshape(t, d)

    wk, wt = _split_weights(ab_w_in[0], q_cols=[(0, wa), (3 * wa, 4 * wa)],
                            k_cols=[(wa, 2 * wa), (4 * wa, 5 * wa)], v_cols=[(2 * wa, 3 * wa), (5 * wa, 6 * wa)])
    q_tiles0 = list(range(n_pairs)) + list(range(2 * n_pairs, 3 * n_pairs))
    kn, qvt = _norm_proj(h, ab_norm_g[0], wk, wt, tabs, q_tiles0)
    oa = _dilated_attention(kn, qvt, b, s, 0, 0, n_pairs, n_pairs)
    ob = _moba_attention(kn, qvt, b, s, 2 * n_pairs, n_pairs, 3 * n_pairs, n_pairs)
    h = _mixer_out_ffn(h, [oa, ob], ab_w_out[0].astype(BF16), ffn_norm_g[0], 0, wg, wu, wd, final_norm_g, False)

    wk, wt = _split_weights(diff_w_in[0], q_cols=[(0, d)], k_cols=[(d, 2 * d)], v_cols=[(2 * d, 3 * d)])
    kn, qvt = _norm_proj(h, diff_norm_g[0], wk, wt, tabs, list(range(n_diff)))
    od = _diff_attention(kn, qvt, b, s, diff_lambda_q1[0], diff_lambda_k1[0],
                         diff_lambda_q2[0], diff_lambda_k2[0], diff_subln_g[0], _lambda_init(1), n_diff)
    h = _mixer_out_ffn(h, [od], diff_w_out[0].astype(BF16), ffn_norm_g[1], 1, wg, wu, wd, final_norm_g, True)
    return h.reshape(b, s, d)
```

```python
import collections
import functools
import math

import numpy as np
import jax
import jax.numpy as jnp
from jax import lax
from jax.experimental import pallas as pl
from jax.experimental.pallas import tpu as pltpu

D_MODEL = 1024
HEAD_DIM = 64
LANES = 128
SUBLANES = 8
ROPE_DIM = HEAD_DIM // 4
ROPE_HALF = ROPE_DIM // 2
ROPE_THETA = 500000.0
D_FF = 2816
NORM_EPS = 1e-5
ATTN_SCALE = HEAD_DIM ** -0.5
LOG2E = math.log2(math.e)
NEG = -1e30
DILATED_CONFIGS = ((128, 1), (512, 4), (2048, 16))
MOBA_BLOCK = 256
MOBA_TOPK = 3

ATT_TILE = 256
ATT_SUB = 2
ROW_TILE = 512
PROJ_ROW_TILE = 1024
VMEM_LIMIT = 56 * 1024 * 1024

F32 = jnp.float32
BF16 = jnp.bfloat16
_NT = (((1,), (1,)), ((), ()))

assert ROPE_HALF == SUBLANES


def _cparams(sem):
    return pltpu.CompilerParams(dimension_semantics=sem, vmem_limit_bytes=VMEM_LIMIT)


def _rope_table_kernel(pos_ref, invf_ref, c_ref, s1_ref, s2_ref, ct_ref, st_ref):
    ang = invf_ref[...] * pos_ref[...].astype(F32)
    cos, sin = jnp.cos(ang), jnp.sin(ang)
    ct_ref[...] = cos
    st_ref[...] = sin
    tm = ang.shape[1]
    one = jnp.ones((HEAD_DIM - ROPE_DIM, tm), F32)
    zero = jnp.zeros((ROPE_HALF, tm), F32)
    zero_rest = jnp.zeros((HEAD_DIM - ROPE_DIM, tm), F32)
    c_ref[...] = jnp.concatenate([cos, cos, one] * 2, axis=0).T
    s1_ref[...] = jnp.concatenate([-sin, zero, zero_rest] * 2, axis=0).T
    s2_ref[...] = jnp.concatenate([zero, sin, zero_rest] * 2, axis=0).T


def _rope_tables(positions):
    t = positions.size
    tm = 1024
    inv_freq = ROPE_THETA ** (-jnp.arange(0, ROPE_DIM, 2, dtype=F32) / ROPE_DIM)
    tab = jax.ShapeDtypeStruct((t, LANES), F32)
    tab_t = jax.ShapeDtypeStruct((ROPE_HALF, t), F32)
    row = lambda i: (i, 0)
    col = lambda i: (0, i)
    return pl.pallas_call(
        _rope_table_kernel,
        grid=(t // tm,),
        in_specs=[pl.BlockSpec((1, tm), col), pl.BlockSpec((ROPE_HALF, 1), lambda i: (0, 0))],
        out_specs=[pl.BlockSpec((tm, LANES), row)] * 3 + [pl.BlockSpec((ROPE_HALF, tm), col)] * 2,
        out_shape=[tab] * 3 + [tab_t] * 2,
        compiler_params=_cparams(("parallel",)),
        name="rope_tables",
    )(positions.reshape(1, t), inv_freq.reshape(ROPE_HALF, 1))


def _norm_proj_kernel(x_ref, g_ref, wk_ref, wt_ref, c_ref, s1_ref, s2_ref, ct_ref, st_ref,
                      k_ref, qvt_ref, acc_ref, *, q_tiles):
    x = x_ref[...]
    ms = jnp.mean(x * x, axis=-1, keepdims=True)
    xn = (x * lax.rsqrt(ms + NORM_EPS) * g_ref[...]).astype(BF16)

    c, s1, s2 = c_ref[...], s1_ref[...], s2_ref[...]
    chunk = 512
    for n in range(wk_ref.shape[1] // chunk):
        acc = jnp.dot(xn, wk_ref[:, n * chunk:(n + 1) * chunk], preferred_element_type=F32)
        for t in range(chunk // LANES):
            seg = acc[:, t * LANES:(t + 1) * LANES]
            seg = seg * c + pltpu.roll(seg, LANES - ROPE_HALF, 1) * s1 + pltpu.roll(seg, ROPE_HALF, 1) * s2
            k_ref[n * (chunk // LANES) + t] = seg.astype(BF16)

    acc_ref[...] = lax.dot_general(wt_ref[...], xn, _NT, preferred_element_type=F32)
    ct, st = ct_ref[...], st_ref[...]
    for r in range(qvt_ref.shape[0] // LANES):
        if r not in q_tiles:
            qvt_ref[r * LANES:(r + 1) * LANES, :] = acc_ref[r * LANES:(r + 1) * LANES, :].astype(BF16)
            continue
        for r0 in range(r * LANES, (r + 1) * LANES, HEAD_DIM):
            x1 = acc_ref[r0:r0 + ROPE_HALF, :]
            x2 = acc_ref[r0 + ROPE_HALF:r0 + ROPE_DIM, :]
            rot = jnp.concatenate([x1 * ct - x2 * st, x2 * ct + x1 * st], axis=0)
            qvt_ref[r0:r0 + ROPE_DIM, :] = rot.astype(BF16)
            qvt_ref[r0 + ROPE_DIM:r0 + HEAD_DIM, :] = acc_ref[r0 + ROPE_DIM:r0 + HEAD_DIM, :].astype(BF16)


def _norm_proj(x, g, wk, wt, tabs, q_tiles):
    t, d = x.shape
    nk, nt = wk.shape[1], wt.shape[0]
    tm = PROJ_ROW_TILE
    row = lambda i: (i, 0)
    col = lambda i: (0, i)
    fixed = lambda i: (0, 0)
    once = dict(pipeline_mode=pl.Buffered(1))
    return pl.pallas_call(
        functools.partial(_norm_proj_kernel, q_tiles=frozenset(q_tiles)),
        grid=(t // tm,),
        in_specs=[pl.BlockSpec((tm, d), row), pl.BlockSpec((1, d), fixed),
                  pl.BlockSpec((d, nk), fixed, **once), pl.BlockSpec((nt, d), fixed, **once),
                  pl.BlockSpec((tm, LANES), row), pl.BlockSpec((tm, LANES), row), pl.BlockSpec((tm, LANES), row),
                  pl.BlockSpec((ROPE_HALF, tm), col), pl.BlockSpec((ROPE_HALF, tm), col)],
        out_specs=[pl.BlockSpec((nk // LANES, tm, LANES), lambda i: (0, i, 0)), pl.BlockSpec((nt, tm), col)],
        out_shape=[jax.ShapeDtypeStruct((nk // LANES, t, LANES), BF16), jax.ShapeDtypeStruct((nt, t), BF16)],
        scratch_shapes=[pltpu.VMEM((nt, tm), F32)],
        compiler_params=_cparams(("parallel",)),
        name="norm_proj_rope",
    )(x, g.reshape(1, d), wk, wt, *tabs)


def _row_masks(shape):
    row = lax.broadcasted_iota(jnp.int32, shape, 0)
    return row < HEAD_DIM, row >= HEAD_DIM


def _causal_table_t(t):
    d = np.arange(t)[None, :] - np.arange(t)[:, None]
    return jnp.asarray(np.where(d >= 0, 0.0, NEG), F32)


def _logits_stage(qt, k_of, bias_of, n_blocks, s_ref):
    t = ATT_TILE
    m = None
    for j in range(n_blocks):
        s = jnp.dot(k_of(j), qt, preferred_element_type=F32)
        bias = bias_of(j)
        if bias is not None:
            s = s + bias
        s_ref[j * t:(j + 1) * t, :] = s
        mj = jnp.max(s, axis=0, keepdims=True)
        m = mj if m is None else jnp.maximum(m, mj)
    return m


def _probs_stage(m, n_blocks, s_ref, p_ref, want_sum):
    t = ATT_TILE
    l = None
    for j in range(n_blocks):
        p = jnp.exp2(s_ref[j * t:(j + 1) * t, :] - m)
        p_ref[j * t:(j + 1) * t, :] = p.astype(BF16)
        if want_sum:
            lj = jnp.sum(p, axis=0, keepdims=True)
            l = lj if l is None else l + lj
    return l


def _values_stage(l, vt_of, n_blocks, p_ref, ones_row):
    n = n_blocks * ATT_TILE
    acc = jnp.dot(vt_of(n), p_ref[:n, :], preferred_element_type=F32)
    if l is None:
        l = acc[ones_row:ones_row + 1, :]
    return acc * (1.0 / l)


_Problem = collections.namedtuple(
    "_Problem", "make_query k_of_head bias_of_tile vt_of_head ones_rows finish_tile")


def _pipelined_attention(n_tiles, problems, s_refs, p_refs):
    items = [(pr, i, h) for i in reversed(range(n_tiles)) for pr in problems for h in range(2)]
    n = len(items)
    ms, ls, outs = {}, {}, {}
    for step in range(n + 2):
        if 1 <= step <= n:
            c = step - 1
            pr, i, h = items[c]
            ls[c] = _probs_stage(ms.pop(c), i + 1, s_refs.at[c % 2], p_refs.at[c % 2], pr.ones_rows[h] is None)
        if step < n:
            pr, i, h = items[step]
            ms[step] = _logits_stage(pr.make_query(i, h), pr.k_of_head(h), pr.bias_of_tile(i), i + 1,
                                     s_refs.at[step % 2])
        if 2 <= step <= n + 1:
            c = step - 2
            pr, i, h = items[c]
            outs[c] = _values_stage(ls.pop(c), pr.vt_of_head(h), i + 1, p_refs.at[c % 2], pr.ones_rows[h])
            if h == 1:
                pr.finish_tile(i, outs.pop(c - 1), outs.pop(c))


def _values_with_ones(vt_ref, vta_ref, vtb_ref):
    vt = vt_ref[...].astype(F32)
    lo, hi = _row_masks(vt.shape)
    vta_ref[...] = jnp.where(lo, vt, 1.0).astype(BF16)
    vtb_ref[...] = jnp.where(hi, vt, 1.0).astype(BF16)
    vt_of = (lambda n: vta_ref[:, :n], lambda n: vtb_ref[:, :n])
    return (lambda h: vt_of[h]), (HEAD_DIM, 0)


def _key_blocks(ref):
    t = ATT_TILE
    return lambda j: ref[j * t:(j + 1) * t, :]


def _tile_rows(u):
    return pl.ds(u * LANES, LANES)


def _attn_scratch(s):
    return [pltpu.VMEM((2, s, ATT_TILE), F32), pltpu.VMEM((2, s, ATT_TILE), BF16)]


def _k_spec(s, tile0):
    assert tile0 % ATT_SUB == 0
    return pl.BlockSpec((ATT_SUB, s, LANES), lambda b, p: (tile0 // ATT_SUB + p, b, 0))


def _t_spec(s, tile0):
    assert tile0 % ATT_SUB == 0
    return pl.BlockSpec((ATT_SUB * LANES, s), lambda b, p: (tile0 // ATT_SUB + p, b))


def _dilated_bias_tables_t(t):
    windows = sorted(w for w, _ in DILATED_CONFIGS)
    far = windows[-2] // t + 1
    tabs = []
    for delta in range(far + 1):
        d = delta * t + np.arange(t)[None, :] - np.arange(t)[:, None]
        mult = np.zeros((t, t), np.float64)
        for window, dil in DILATED_CONFIGS:
            mult += (d >= 0) & (d % dil == 0) & (d <= window)
        tabs.append(np.where(mult > 0, np.log2(np.maximum(mult, 1.0)), NEG))
    return jnp.asarray(np.stack(tabs), F32)


def _dilated_problem(qt_ref, k_ref, vt_ref, bias_ref, o_ref, vta_ref, vtb_ref):
    t = ATT_TILE
    n_tab = bias_ref.shape[0]
    k_of = _key_blocks(k_ref)
    vt_of_head, ones_rows = _values_with_ones(vt_ref, vta_ref, vtb_ref)

    def make_query(i, h):
        qt = qt_ref[:, i * t:(i + 1) * t].astype(F32)
        return jnp.where(_row_masks(qt.shape)[h], qt, 0.0).astype(BF16)

    def finish_tile(i, out_lo, out_hi):
        lo, _ = _row_masks(out_lo.shape)
        o_ref[i * t:(i + 1) * t, :] = jnp.where(lo, out_lo, out_hi).T.astype(BF16)

    return _Problem(make_query, lambda h: k_of, lambda i: (lambda j: bias_ref[min(i - j, n_tab - 1)]),
                    vt_of_head, ones_rows, finish_tile)


def _dilated_kernel(qt_ref, k_ref, vt_ref, bias_ref, o_ref, vta_ref, vtb_ref, s_refs, p_refs):
    problems = [_dilated_problem(qt_ref.at[_tile_rows(u)], k_ref.at[u], vt_ref.at[_tile_rows(u)], bias_ref,
                                 o_ref.at[u], vta_ref.at[u], vtb_ref.at[u]) for u in range(ATT_SUB)]
    _pipelined_attention(k_ref.shape[1] // ATT_TILE, problems, s_refs, p_refs)


def _ones_value_scratch(s):
    return [pltpu.VMEM((ATT_SUB, LANES, s), BF16), pltpu.VMEM((ATT_SUB, LANES, s), BF16)]


def _dilated_attention(kn, qvt, b, s, q_tile0, k_tile0, v_tile0, n_pairs):
    bias = _dilated_bias_tables_t(ATT_TILE)
    return pl.pallas_call(
        _dilated_kernel,
        grid=(b, n_pairs // ATT_SUB),
        in_specs=[_t_spec(s, q_tile0), _k_spec(s, k_tile0), _t_spec(s, v_tile0),
                  pl.BlockSpec(bias.shape, lambda b_, p: (0, 0, 0))],
        out_specs=_k_spec(s, 0),
        out_shape=jax.ShapeDtypeStruct((n_pairs, b * s, LANES), BF16),
        scratch_shapes=_ones_value_scratch(s) + _attn_scratch(s),
        compiler_params=_cparams(("parallel", "parallel")),
        name="dilated_attention",
    )(qvt, kn, qvt, bias)


def _moba_problem(qt_ref, k_ref, vt_ref, causal_ref, o_ref, ka_ref, kb_ref, vta_ref, vtb_ref):
    t = MOBA_BLOCK
    vt_of_head, ones_rows = _values_with_ones(vt_ref, vta_ref, vtb_ref)
    s_len = k_ref.shape[0]
    nblk = s_len // t

    k = k_ref[...].astype(F32)
    lane_k = lax.broadcasted_iota(jnp.int32, k.shape, 1)
    blk = lax.broadcasted_iota(jnp.int32, k.shape, 0) // t
    ka_ref[...] = jnp.where(lane_k < HEAD_DIM, k, jnp.where(lane_k - HEAD_DIM == blk, 1.0, 0.0)).astype(BF16)
    kb_ref[...] = jnp.where(lane_k >= HEAD_DIM, k, jnp.where(lane_k == blk, 1.0, 0.0)).astype(BF16)
    means = jnp.concatenate(
        [jnp.mean(k_ref[n * t:(n + 1) * t, :].astype(F32), axis=0, keepdims=True) for n in range(nblk)], axis=0)
    lane_m = lax.broadcasted_iota(jnp.int32, means.shape, 1)
    means2 = jnp.concatenate([jnp.where(lane_m < HEAD_DIM, means, 0.0),
                              jnp.where(lane_m >= HEAD_DIM, means, 0.0)], axis=0)

    zeros = jnp.zeros((HEAD_DIM - nblk, t), F32)
    k_of = (_key_blocks(ka_ref), _key_blocks(kb_ref))

    def make_query(i, h):
        qt = qt_ref[:, i * t:(i + 1) * t].astype(F32)
        if i <= MOBA_TOPK:
            return jnp.where(_row_masks(qt.shape)[h], qt, 0.0).astype(BF16)
        g = jnp.dot(means2[h * nblk:(h + 1) * nblk], qt, preferred_element_type=F32,
                    precision=lax.Precision.HIGHEST)
        row = lax.broadcasted_iota(jnp.int32, g.shape, 0)
        rank = jnp.zeros(g.shape, F32)
        for kk in range(i):
            gk = g[kk:kk + 1, :]
            beats = (gk > g) | ((gk == g) & (row > kk))
            rank = rank + jnp.where(beats, 1.0, 0.0)
        drop = jnp.where((row < i) & (rank >= MOBA_TOPK), NEG, 0.0)
        parts = [qt[:HEAD_DIM], drop, zeros] if h == 0 else [drop, zeros, qt[HEAD_DIM:]]
        return jnp.concatenate(parts, axis=0).astype(BF16)

    def finish_tile(i, out_lo, out_hi):
        lo, _ = _row_masks(out_lo.shape)
        o_ref[i * t:(i + 1) * t, :] = jnp.where(lo, out_lo, out_hi).T.astype(BF16)

    return _Problem(make_query, lambda h: k_of[h], lambda i: (lambda j: causal_ref[...] if j == i else None),
                    vt_of_head, ones_rows, finish_tile)


def _moba_kernel(qt_ref, k_ref, vt_ref, causal_ref, o_ref, ka_ref, kb_ref, vta_ref, vtb_ref, s_refs, p_refs):
    problems = [_moba_problem(qt_ref.at[_tile_rows(u)], k_ref.at[u], vt_ref.at[_tile_rows(u)], causal_ref,
                              o_ref.at[u], ka_ref.at[u], kb_ref.at[u], vta_ref.at[u], vtb_ref.at[u])
                for u in range(ATT_SUB)]
    _pipelined_attention(k_ref.shape[1] // ATT_TILE, problems, s_refs, p_refs)


def _moba_attention(kn, qvt, b, s, q_tile0, k_tile0, v_tile0, n_pairs):
    t = MOBA_BLOCK
    assert t == ATT_TILE and s // t <= SUBLANES
    return pl.pallas_call(
        _moba_kernel,
        grid=(b, n_pairs // ATT_SUB),
        in_specs=[_t_spec(s, q_tile0), _k_spec(s, k_tile0), _t_spec(s, v_tile0),
                  pl.BlockSpec((t, t), lambda b_, p: (0, 0))],
        out_specs=_k_spec(s, 0),
        out_shape=jax.ShapeDtypeStruct((n_pairs, b * s, LANES), BF16),
        scratch_shapes=[pltpu.VMEM((ATT_SUB, s, LANES), BF16), pltpu.VMEM((ATT_SUB, s, LANES), BF16)]
        + _ones_value_scratch(s) + _attn_scratch(s),
        compiler_params=_cparams(("parallel", "parallel")),
        name="moba_attention",
    )(qvt, kn, qvt, _causal_table_t(t))


def _diff_problem(qt_ref, k_ref, vt_ref, causal_ref, lam, g_ref, o_ref, lambda_init):
    t = ATT_TILE
    k_of = _key_blocks(k_ref)

    def make_query(i, h):
        qt = qt_ref[:, i * t:(i + 1) * t].astype(F32)
        return jnp.where(_row_masks(qt.shape)[h], qt, 0.0).astype(BF16)

    def finish_tile(i, out1, out2):
        o = out1 - lam * out2
        y = o * lax.rsqrt(jnp.mean(o * o, axis=0, keepdims=True) + NORM_EPS)
        o_ref[i * t:(i + 1) * t, :] = ((y.T * g_ref[...]) * (1.0 - lambda_init)).astype(BF16)

    return _Problem(make_query, lambda h: k_of, lambda i: (lambda j: causal_ref[...] if j == i else None),
                    lambda h: (lambda n: vt_ref[:, :n]), (None, None), finish_tile)


def _diff_kernel(qt_ref, k_ref, vt_ref, causal_ref, lq1_ref, lk1_ref, lq2_ref, lk2_ref, g_ref, o_ref,
                 s_refs, p_refs, *, lambda_init):
    lam = (jnp.exp(jnp.sum(lq1_ref[...] * lk1_ref[...], axis=-1, keepdims=True))
           - jnp.exp(jnp.sum(lq2_ref[...] * lk2_ref[...], axis=-1, keepdims=True)) + lambda_init)
    problems = [_diff_problem(qt_ref.at[_tile_rows(u)], k_ref.at[u], vt_ref.at[_tile_rows(u)], causal_ref, lam,
                              g_ref, o_ref.at[u], lambda_init) for u in range(ATT_SUB)]
    _pipelined_attention(k_ref.shape[1] // ATT_TILE, problems, s_refs, p_refs)


def _diff_attention(kn, qvt, b, s, lq1, lk1, lq2, lk2, subln_g, lambda_init, n_heads):
    t = ATT_TILE
    vec = lambda a: a.reshape(1, -1).astype(F32)
    small = lambda n: pl.BlockSpec((1, n), lambda b_, h: (0, 0))
    return pl.pallas_call(
        functools.partial(_diff_kernel, lambda_init=lambda_init),
        grid=(b, n_heads // ATT_SUB),
        in_specs=[_t_spec(s, 0), _k_spec(s, 0), _t_spec(s, n_heads), pl.BlockSpec((t, t), lambda b_, h: (0, 0)),
                  small(HEAD_DIM), small(HEAD_DIM), small(HEAD_DIM), small(HEAD_DIM), small(LANES)],
        out_specs=_k_spec(s, 0),
        out_shape=jax.ShapeDtypeStruct((n_heads, b * s, LANES), BF16),
        scratch_shapes=_attn_scratch(s),
        compiler_params=_cparams(("parallel", "parallel")),
        name="diff_attention",
    )(qvt, kn, qvt, _causal_table_t(t), vec(lq1), vec(lk1), vec(lq2), vec(lk2), vec(subln_g))


def _mixer_out_ffn_kernel(*refs, n_acts, chunk, final_norm):
    h_ref = refs[0]
    act_refs = refs[1:1 + n_acts]
    wo_ref, g_ref, wg_ref, wu_ref, wd_ref, gf_ref, o_ref = refs[1 + n_acts:]
    act = jnp.concatenate([a_ref[c] for a_ref in act_refs for c in range(a_ref.shape[0])], axis=1)
    x = h_ref[...] + jnp.dot(act, wo_ref[...], preferred_element_type=F32)
    ms = jnp.mean(x * x, axis=-1, keepdims=True)
    xn = (x * lax.rsqrt(ms + NORM_EPS) * g_ref[...]).astype(BF16)
    acc = x
    for c in range(wg_ref.shape[1] // chunk):
        sl = slice(c * chunk, (c + 1) * chunk)
        gate = jnp.dot(xn, wg_ref[:, sl], preferred_element_type=F32)
        up = jnp.dot(xn, wu_ref[:, sl], preferred_element_type=F32)
        mid = (gate * jax.nn.sigmoid(gate) * up).astype(BF16)
        acc = acc + jnp.dot(mid, wd_ref[sl, :], preferred_element_type=F32)
    if final_norm:
        ms = jnp.mean(acc * acc, axis=-1, keepdims=True)
        acc = acc * lax.rsqrt(ms + NORM_EPS) * gf_ref[...]
    o_ref[...] = acc


def _mixer_out_ffn(h, acts, w_out, g, layer, wg, wu, wd, g_final, final_norm):
    t, d = h.shape
    ff = wg.shape[2]
    tm = ROW_TILE
    row = lambda i: (i, 0)
    fixed = lambda i: (0, 0)
    of_layer = lambda i: (layer, 0, 0)
    once = dict(pipeline_mode=pl.Buffered(1))
    in_specs = [pl.BlockSpec((tm, d), row)]
    in_specs += [pl.BlockSpec((a.shape[0], tm, LANES), lambda i: (0, i, 0)) for a in acts]
    in_specs += [pl.BlockSpec(w_out.shape, fixed, **once), pl.BlockSpec((1, d), fixed),
                 pl.BlockSpec((None, d, ff), of_layer, **once), pl.BlockSpec((None, d, ff), of_layer, **once),
                 pl.BlockSpec((None, ff, d), of_layer, **once), pl.BlockSpec((1, d), fixed)]
    args = [h, *acts, w_out, g.reshape(1, d), wg, wu, wd, g_final.reshape(1, d)]
    return pl.pallas_call(
        functools.partial(_mixer_out_ffn_kernel, n_acts=len(acts), chunk=256, final_norm=final_norm),
        grid=(t // tm,),
        in_specs=in_specs,
        out_specs=pl.BlockSpec((tm, d), row),
        out_shape=jax.ShapeDtypeStruct((t, d), F32),
        compiler_params=_cparams(("parallel",)),
        name="mixer_out_ffn",
    )(*args)


def _cast_kernel(*refs):
    n = len(refs) // 2
    for src_ref, dst_ref in zip(refs[:n], refs[n:]):
        dst_ref[...] = src_ref[...].astype(BF16)


def _cast_bf16(arrays, n_chunks):
    specs = [pl.BlockSpec((1, a.shape[1] // n_chunks, a.shape[2]), lambda l, c: (l, c, 0)) for a in arrays]
    return pl.pallas_call(
        _cast_kernel,
        grid=(arrays[0].shape[0], n_chunks),
        in_specs=specs,
        out_specs=specs,
        out_shape=[jax.ShapeDtypeStruct(a.shape, BF16) for a in arrays],
        compiler_params=_cparams(("parallel", "parallel")),
        name="cast_weights",
    )(*arrays)


def _lambda_init(layer_idx):
    return 0.8 - 0.6 * math.exp(-0.3 * layer_idx)


def _split_weights(w_in, q_cols, k_cols, v_cols):
    cat = lambda cols: jnp.concatenate([w_in[:, a:b] for a, b in cols], axis=1)
    groups = [(cat([c]) * (ATTN_SCALE * LOG2E) if is_q else cat([c])) for c, is_q in
              sorted([(c, True) for c in q_cols] + [(c, False) for c in v_cols])]
    return cat(k_cols).astype(BF16), jnp.concatenate(groups, axis=1).T.astype(BF16)


def kernel(x, positions, ab_norm_g, ab_w_in, ab_w_out, diff_norm_g, diff_w_in, diff_w_out, diff_lambda_q1, diff_lambda_k1, diff_lambda_q2, diff_lambda_k2, diff_subln_g, ffn_norm_g, ffn_w_gate, ffn_w_up, ffn_w_down, final_norm_g):
    b, s, d = x.shape
    t = b * s
    n_pairs = d // (4 * HEAD_DIM)
    n_diff = d // (2 * HEAD_DIM)
    wa = n_pairs * LANES
    tabs = _rope_tables(positions)
    wg, wu, wd = _cast_bf16([ffn_w_gate, ffn_w_up, ffn_w_down], n_chunks=4)
    h = x.reshape(t, d)

    wk, wt = _split_weights(ab_w_in[0], q_cols=[(0, wa), (3 * wa, 4 * wa)],
                            k_cols=[(wa, 2 * wa), (4 * wa, 5 * wa)], v_cols=[(2 * wa, 3 * wa), (5 * wa, 6 * wa)])
    q_tiles0 = list(range(n_pairs)) + list(range(2 * n_pairs, 3 * n_pairs))
    kn, qvt = _norm_proj(h, ab_norm_g[0], wk, wt, tabs, q_tiles0)
    oa = _dilated_attention(kn, qvt, b, s, 0, 0, n_pairs, n_pairs)
    ob = _moba_attention(kn, qvt, b, s, 2 * n_pairs, n_pairs, 3 * n_pairs, n_pairs)
    h = _mixer_out_ffn(h, [oa, ob], ab_w_out[0].astype(BF16), ffn_norm_g[0], 0, wg, wu, wd, final_norm_g, False)

    wk, wt = _split_weights(diff_w_in[0], q_cols=[(0, d)], k_cols=[(d, 2 * d)], v_cols=[(2 * d, 3 * d)])
    kn, qvt = _norm_proj(h, diff_norm_g[0], wk, wt, tabs, list(range(n_diff)))
    od = _diff_attention(kn, qvt, b, s, diff_lambda_q1[0], diff_lambda_k1[0],
                         diff_lambda_q2[0], diff_lambda_k2[0], diff_subln_g[0], _lambda_init(1), n_diff)
    h = _mixer_out_ffn(h, [od], diff_w_out[0].astype(BF16), ffn_norm_g[1], 1, wg, wu, wd, final_norm_g, True)
    return h.reshape(b, s, d)
```

```python
import collections
import functools
import math

import numpy as np
import jax
import jax.numpy as jnp
from jax import lax
from jax.experimental import pallas as pl
from jax.experimental.pallas import tpu as pltpu

D_MODEL = 1024
HEAD_DIM = 64
LANES = 128
SUBLANES = 8
ROPE_DIM = HEAD_DIM // 4
ROPE_HALF = ROPE_DIM // 2
ROPE_THETA = 500000.0
D_FF = 2816
NORM_EPS = 1e-5
ATTN_SCALE = HEAD_DIM ** -0.5
LOG2E = math.log2(math.e)
NEG = -1e30
DILATED_CONFIGS = ((128, 1), (512, 4), (2048, 16))
MOBA_BLOCK = 256
MOBA_TOPK = 3

ATT_TILE = 256
ATT_SUB = 2
ROW_TILE = 512
PROJ_ROW_TILE = 1024
VMEM_LIMIT = 56 * 1024 * 1024

F32 = jnp.float32
BF16 = jnp.bfloat16
_NT = (((1,), (1,)), ((), ()))

assert ROPE_HALF == SUBLANES


def _cparams(sem):
    return pltpu.CompilerParams(dimension_semantics=sem, vmem_limit_bytes=VMEM_LIMIT)


def _rope_table_kernel(pos_ref, invf_ref, c_ref, s1_ref, s2_ref, ct_ref, st_ref):
    ang = invf_ref[...] * pos_ref[...].astype(F32)
    cos, sin = jnp.cos(ang), jnp.sin(ang)
    ct_ref[...] = cos
    st_ref[...] = sin
    tm = ang.shape[1]
    one = jnp.ones((HEAD_DIM - ROPE_DIM, tm), F32)
    zero = jnp.zeros((ROPE_HALF, tm), F32)
    zero_rest = jnp.zeros((HEAD_DIM - ROPE_DIM, tm), F32)
    c_ref[...] = jnp.concatenate([cos, cos, one] * 2, axis=0).T
    s1_ref[...] = jnp.concatenate([-sin, zero, zero_rest] * 2, axis=0).T
    s2_ref[...] = jnp.concatenate([zero, sin, zero_rest] * 2, axis=0).T


def _rope_tables(positions):
    t = positions.size
    tm = 1024
    inv_freq = ROPE_THETA ** (-jnp.arange(0, ROPE_DIM, 2, dtype=F32) / ROPE_DIM)
    tab = jax.ShapeDtypeStruct((t, LANES), F32)
    tab_t = jax.ShapeDtypeStruct((ROPE_HALF, t), F32)
    row = lambda i: (i, 0)
    col = lambda i: (0, i)
    return pl.pallas_call(
        _rope_table_kernel,
        grid=(t // tm,),
        in_specs=[pl.BlockSpec((1, tm), col), pl.BlockSpec((ROPE_HALF, 1), lambda i: (0, 0))],
        out_specs=[pl.BlockSpec((tm, LANES), row)] * 3 + [pl.BlockSpec((ROPE_HALF, tm), col)] * 2,
        out_shape=[tab] * 3 + [tab_t] * 2,
        compiler_params=_cparams(("parallel",)),
        name="rope_tables",
    )(positions.reshape(1, t), inv_freq.reshape(ROPE_HALF, 1))


def _norm_proj_kernel(x_ref, g_ref, wk_ref, wt_ref, c_ref, s1_ref, s2_ref, ct_ref, st_ref,
                      k_ref, qvt_ref, acc_ref, *, q_tiles):
    x = x_ref[...]
    ms = jnp.mean(x * x, axis=-1, keepdims=True)
    xn = (x * lax.rsqrt(ms + NORM_EPS) * g_ref[...]).astype(BF16)

    c, s1, s2 = c_ref[...], s1_ref[...], s2_ref[...]
    chunk = 512
    for n in range(wk_ref.shape[1] // chunk):
        acc = jnp.dot(xn, wk_ref[:, n * chunk:(n + 1) * chunk], preferred_element_type=F32)
        for t in range(chunk // LANES):
            seg = acc[:, t * LANES:(t + 1) * LANES]
            seg = seg * c + pltpu.roll(seg, LANES - ROPE_HALF, 1) * s1 + pltpu.roll(seg, ROPE_HALF, 1) * s2
            k_ref[n * (chunk // LANES) + t] = seg.astype(BF16)

    acc_ref[...] = lax.dot_general(wt_ref[...], xn, _NT, preferred_element_type=F32)
    ct, st = ct_ref[...], st_ref[...]
    for r in range(qvt_ref.shape[0] // LANES):
        if r not in q_tiles:
            qvt_ref[r * LANES:(r + 1) * LANES, :] = acc_ref[r * LANES:(r + 1) * LANES, :].astype(BF16)
            continue
        for r0 in range(r * LANES, (r + 1) * LANES, HEAD_DIM):
            x1 = acc_ref[r0:r0 + ROPE_HALF, :]
            x2 = acc_ref[r0 + ROPE_HALF:r0 + ROPE_DIM, :]
            rot = jnp.concatenate([x1 * ct - x2 * st, x2 * ct + x1 * st], axis=0)
            qvt_ref[r0:r0 + ROPE_DIM, :] = rot.astype(BF16)
            qvt_ref[r0 + ROPE_DIM:r0 + HEAD_DIM, :] = acc_ref[r0 + ROPE_DIM:r0 + HEAD_DIM, :].astype(BF16)


def _norm_proj(x, g, wk, wt, tabs, q_tiles):
    t, d = x.shape
    nk, nt = wk.shape[1], wt.shape[0]
    tm = PROJ_ROW_TILE
    row = lambda i: (i, 0)
    col = lambda i: (0, i)
    fixed = lambda i: (0, 0)
    once = dict(pipeline_mode=pl.Buffered(1))
    return pl.pallas_call(
        functools.partial(_norm_proj_kernel, q_tiles=frozenset(q_tiles)),
        grid=(t // tm,),
        in_specs=[pl.BlockSpec((tm, d), row), pl.BlockSpec((1, d), fixed),
                  pl.BlockSpec((d, nk), fixed, **once), pl.BlockSpec((nt, d), fixed, **once),
                  pl.BlockSpec((tm, LANES), row), pl.BlockSpec((tm, LANES), row), pl.BlockSpec((tm, LANES), row),
                  pl.BlockSpec((ROPE_HALF, tm), col), pl.BlockSpec((ROPE_HALF, tm), col)],
        out_specs=[pl.BlockSpec((nk // LANES, tm, LANES), lambda i: (0, i, 0)), pl.BlockSpec((nt, tm), col)],
        out_shape=[jax.ShapeDtypeStruct((nk // LANES, t, LANES), BF16), jax.ShapeDtypeStruct((nt, t), BF16)],
        scratch_shapes=[pltpu.VMEM((nt, tm), F32)],
        compiler_params=_cparams(("parallel",)),
        name="norm_proj_rope",
    )(x, g.reshape(1, d), wk, wt, *tabs)


def _row_masks(shape):
    row = lax.broadcasted_iota(jnp.int32, shape, 0)
    return row < HEAD_DIM, row >= HEAD_DIM


def _causal_table_t(t):
    d = np.arange(t)[None, :] - np.arange(t)[:, None]
    return jnp.asarray(np.where(d >= 0, 0.0, NEG), F32)


def _logits_stage(qt, k_of, bias_of, n_blocks, s_ref):
    t = ATT_TILE
    m = None
    for j in range(n_blocks):
        s = jnp.dot(k_of(j), qt, preferred_element_type=F32)
        bias = bias_of(j)
        if bias is not None:
            s = s + bias
        s_ref[j * t:(j + 1) * t, :] = s
        mj = jnp.max(s, axis=0, keepdims=True)
        m = mj if m is None else jnp.maximum(m, mj)
    return m


def _probs_stage(m, n_blocks, s_ref, p_ref, want_sum):
    t = ATT_TILE
    l = None
    for j in range(n_blocks):
        p = jnp.exp2(s_ref[j * t:(j + 1) * t, :] - m)
        p_ref[j * t:(j + 1) * t, :] = p.astype(BF16)
        if want_sum:
            lj = jnp.sum(p, axis=0, keepdims=True)
            l = lj if l is None else l + lj
    return l


def _values_stage(l, vt_of, n_blocks, p_ref, ones_row):
    n = n_blocks * ATT_TILE
    acc = jnp.dot(vt_of(n), p_ref[:n, :], preferred_element_type=F32)
    if l is None:
        l = acc[ones_row:ones_row + 1, :]
    return acc * (1.0 / l)


_Problem = collections.namedtuple(
    "_Problem", "make_query k_of_head bias_of_tile vt_of_head ones_rows finish_tile")


def _pipelined_attention(n_tiles, problems, s_refs, p_refs):
    items = [(pr, i, h) for i in reversed(range(n_tiles)) for pr in problems for h in range(2)]
    n = len(items)
    ms, ls, outs = {}, {}, {}
    for step in range(n + 2):
        if 1 <= step <= n:
            c = step - 1
            pr, i, h = items[c]
            ls[c] = _probs_stage(ms.pop(c), i + 1, s_refs.at[c % 2], p_refs.at[c % 2], pr.ones_rows[h] is None)
        if step < n:
            pr, i, h = items[step]
            ms[step] = _logits_stage(pr.make_query(i, h), pr.k_of_head(h), pr.bias_of_tile(i), i + 1,
                                     s_refs.at[step % 2])
        if 2 <= step <= n + 1:
            c = step - 2
            pr, i, h = items[c]
            outs[c] = _values_stage(ls.pop(c), pr.vt_of_head(h), i + 1, p_refs.at[c % 2], pr.ones_rows[h])
            if h == 1:
                pr.finish_tile(i, outs.pop(c - 1), outs.pop(c))


def _values_with_ones(vt_ref, vta_ref, vtb_ref):
    ones = jnp.ones((HEAD_DIM, vt_ref.shape[1]), BF16)
    vta_ref[:HEAD_DIM, :] = vt_ref[:HEAD_DIM, :]
    vta_ref[HEAD_DIM:, :] = ones
    vtb_ref[:HEAD_DIM, :] = ones
    vtb_ref[HEAD_DIM:, :] = vt_ref[HEAD_DIM:, :]
    vt_of = (lambda n: vta_ref[:, :n], lambda n: vtb_ref[:, :n])
    return (lambda h: vt_of[h]), (HEAD_DIM, 0)


def _key_blocks(ref):
    t = ATT_TILE
    return lambda j: ref[j * t:(j + 1) * t, :]


def _tile_rows(u):
    return pl.ds(u * LANES, LANES)


def _attn_scratch(s):
    return [pltpu.VMEM((2, s, ATT_TILE), F32), pltpu.VMEM((2, s, ATT_TILE), BF16)]


def _k_spec(s, tile0):
    assert tile0 % ATT_SUB == 0
    return pl.BlockSpec((ATT_SUB, s, LANES), lambda b, p: (tile0 // ATT_SUB + p, b, 0))


def _t_spec(s, tile0):
    assert tile0 % ATT_SUB == 0
    return pl.BlockSpec((ATT_SUB * LANES, s), lambda b, p: (tile0 // ATT_SUB + p, b))


def _dilated_bias_tables_t(t):
    windows = sorted(w for w, _ in DILATED_CONFIGS)
    far = windows[-2] // t + 1
    tabs = []
    for delta in range(far + 1):
        d = delta * t + np.arange(t)[None, :] - np.arange(t)[:, None]
        mult = np.zeros((t, t), np.float64)
        for window, dil in DILATED_CONFIGS:
            mult += (d >= 0) & (d % dil == 0) & (d <= window)
        tabs.append(np.where(mult > 0, np.log2(np.maximum(mult, 1.0)), NEG))
    return jnp.asarray(np.stack(tabs), F32)


def _dilated_problem(qt_ref, k_ref, vt_ref, bias_ref, o_ref, vta_ref, vtb_ref):
    t = ATT_TILE
    n_tab = bias_ref.shape[0]
    k_of = _key_blocks(k_ref)
    vt_of_head, ones_rows = _values_with_ones(vt_ref, vta_ref, vtb_ref)

    def make_query(i, h):
        qt = qt_ref[:, i * t:(i + 1) * t].astype(F32)
        return jnp.where(_row_masks(qt.shape)[h], qt, 0.0).astype(BF16)

    def finish_tile(i, out_lo, out_hi):
        lo, _ = _row_masks(out_lo.shape)
        o_ref[i * t:(i + 1) * t, :] = jnp.where(lo, out_lo, out_hi).T.astype(BF16)

    return _Problem(make_query, lambda h: k_of, lambda i: (lambda j: bias_ref[min(i - j, n_tab - 1)]),
                    vt_of_head, ones_rows, finish_tile)


def _dilated_kernel(qt_ref, k_ref, vt_ref, bias_ref, o_ref, vta_ref, vtb_ref, s_refs, p_refs):
    problems = [_dilated_problem(qt_ref.at[_tile_rows(u)], k_ref.at[u], vt_ref.at[_tile_rows(u)], bias_ref,
                                 o_ref.at[u], vta_ref.at[u], vtb_ref.at[u]) for u in range(ATT_SUB)]
    _pipelined_attention(k_ref.shape[1] // ATT_TILE, problems, s_refs, p_refs)


def _ones_value_scratch(s):
    return [pltpu.VMEM((ATT_SUB, LANES, s), BF16), pltpu.VMEM((ATT_SUB, LANES, s), BF16)]


def _dilated_attention(kn, qvt, b, s, q_tile0, k_tile0, v_tile0, n_pairs):
    bias = _dilated_bias_tables_t(ATT_TILE)
    return pl.pallas_call(
        _dilated_kernel,
        grid=(b, n_pairs // ATT_SUB),
        in_specs=[_t_spec(s, q_tile0), _k_spec(s, k_tile0), _t_spec(s, v_tile0),
                  pl.BlockSpec(bias.shape, lambda b_, p: (0, 0, 0))],
        out_specs=_k_spec(s, 0),
        out_shape=jax.ShapeDtypeStruct((n_pairs, b * s, LANES), BF16),
        scratch_shapes=_ones_value_scratch(s) + _attn_scratch(s),
        compiler_params=_cparams(("parallel", "parallel")),
        name="dilated_attention",
    )(qvt, kn, qvt, bias)


def _moba_block_onehots(s):
    blk = np.arange(s) // MOBA_BLOCK
    onehot = np.zeros((2, s, LANES), np.float32)
    onehot[0, np.arange(s), HEAD_DIM + blk] = 1.0
    onehot[1, np.arange(s), blk] = 1.0
    mask = np.zeros((2, 1, LANES), np.float32)
    mask[0, 0, :HEAD_DIM] = 1.0
    mask[1, 0, HEAD_DIM:] = 1.0
    return jnp.asarray(onehot, BF16), jnp.asarray(mask, BF16)


def _moba_problem(qt_ref, k_ref, vt_ref, causal_ref, onehot_ref, mask_ref, o_ref, ka_ref, kb_ref, vta_ref, vtb_ref):
    t = MOBA_BLOCK
    vt_of_head, ones_rows = _values_with_ones(vt_ref, vta_ref, vtb_ref)
    s_len = k_ref.shape[0]
    nblk = s_len // t

    ka_ref[...] = k_ref[...] * mask_ref[0] + onehot_ref[0]
    kb_ref[...] = k_ref[...] * mask_ref[1] + onehot_ref[1]
    means = jnp.concatenate(
        [jnp.mean(k_ref[n * t:(n + 1) * t, :].astype(F32), axis=0, keepdims=True) for n in range(nblk)], axis=0)
    lane_m = lax.broadcasted_iota(jnp.int32, means.shape, 1)
    means2 = jnp.concatenate([jnp.where(lane_m < HEAD_DIM, means, 0.0),
                              jnp.where(lane_m >= HEAD_DIM, means, 0.0)], axis=0)

    hi = means2.astype(BF16)
    rest = means2 - hi.astype(F32)
    mid = rest.astype(BF16)
    lo = (rest - mid.astype(F32)).astype(BF16)
    means_split = jnp.concatenate([hi, mid, lo], axis=0)
    gates = {}

    def gate_scores(i):
        if i not in gates:
            g3 = jnp.dot(means_split, qt_ref[:, i * t:(i + 1) * t], preferred_element_type=F32)
            gates[i] = g3[:2 * nblk] + g3[2 * nblk:4 * nblk] + g3[4 * nblk:]
        return gates[i]

    zeros = jnp.zeros((HEAD_DIM - nblk, t), F32)
    k_of = (_key_blocks(ka_ref), _key_blocks(kb_ref))

    def make_query(i, h):
        qt = qt_ref[:, i * t:(i + 1) * t].astype(F32)
        if i <= MOBA_TOPK:
            return jnp.where(_row_masks(qt.shape)[h], qt, 0.0).astype(BF16)
        g = gate_scores(i)[h * nblk:(h + 1) * nblk]
        row = lax.broadcasted_iota(jnp.int32, g.shape, 0)
        rank = jnp.zeros(g.shape, F32)
        for kk in range(i):
            gk = g[kk:kk + 1, :]
            beats = (gk > g) | ((gk == g) & (row > kk))
            rank = rank + jnp.where(beats, 1.0, 0.0)
        drop = jnp.where((row < i) & (rank >= MOBA_TOPK), NEG, 0.0)
        parts = [qt[:HEAD_DIM], drop, zeros] if h == 0 else [drop, zeros, qt[HEAD_DIM:]]
        return jnp.concatenate(parts, axis=0).astype(BF16)

    def finish_tile(i, out_lo, out_hi):
        lo, _ = _row_masks(out_lo.shape)
        o_ref[i * t:(i + 1) * t, :] = jnp.where(lo, out_lo, out_hi).T.astype(BF16)

    return _Problem(make_query, lambda h: k_of[h], lambda i: (lambda j: causal_ref[...] if j == i else None),
                    vt_of_head, ones_rows, finish_tile)


def _moba_kernel(qt_ref, k_ref, vt_ref, causal_ref, onehot_ref, mask_ref, o_ref, ka_ref, kb_ref, vta_ref, vtb_ref,
                 s_refs, p_refs):
    problems = [_moba_problem(qt_ref.at[_tile_rows(u)], k_ref.at[u], vt_ref.at[_tile_rows(u)], causal_ref,
                              onehot_ref, mask_ref, o_ref.at[u], ka_ref.at[u], kb_ref.at[u], vta_ref.at[u],
                              vtb_ref.at[u]) for u in range(ATT_SUB)]
    _pipelined_attention(k_ref.shape[1] // ATT_TILE, problems, s_refs, p_refs)


def _moba_attention(kn, qvt, b, s, q_tile0, k_tile0, v_tile0, n_pairs):
    t = MOBA_BLOCK
    assert t == ATT_TILE and s // t <= SUBLANES
    onehot, mask = _moba_block_onehots(s)
    fixed3 = lambda b_, p: (0, 0, 0)
    return pl.pallas_call(
        _moba_kernel,
        grid=(b, n_pairs // ATT_SUB),
        in_specs=[_t_spec(s, q_tile0), _k_spec(s, k_tile0), _t_spec(s, v_tile0),
                  pl.BlockSpec((t, t), lambda b_, p: (0, 0)),
                  pl.BlockSpec(onehot.shape, fixed3), pl.BlockSpec(mask.shape, fixed3)],
        out_specs=_k_spec(s, 0),
        out_shape=jax.ShapeDtypeStruct((n_pairs, b * s, LANES), BF16),
        scratch_shapes=[pltpu.VMEM((ATT_SUB, s, LANES), BF16), pltpu.VMEM((ATT_SUB, s, LANES), BF16)]
        + _ones_value_scratch(s) + _attn_scratch(s),
        compiler_params=_cparams(("parallel", "parallel")),
        name="moba_attention",
    )(qvt, kn, qvt, _causal_table_t(t), onehot, mask)


def _diff_problem(qt_ref, k_ref, vt_ref, causal_ref, lam, g_ref, o_ref, lambda_init):
    t = ATT_TILE
    k_of = _key_blocks(k_ref)

    def make_query(i, h):
        qt = qt_ref[:, i * t:(i + 1) * t].astype(F32)
        return jnp.where(_row_masks(qt.shape)[h], qt, 0.0).astype(BF16)

    def finish_tile(i, out1, out2):
        o = out1 - lam * out2
        y = o * lax.rsqrt(jnp.mean(o * o, axis=0, keepdims=True) + NORM_EPS)
        o_ref[i * t:(i + 1) * t, :] = ((y.T * g_ref[...]) * (1.0 - lambda_init)).astype(BF16)

    return _Problem(make_query, lambda h: k_of, lambda i: (lambda j: causal_ref[...] if j == i else None),
                    lambda h: (lambda n: vt_ref[:, :n]), (None, None), finish_tile)


def _diff_kernel(qt_ref, k_ref, vt_ref, causal_ref, lq1_ref, lk1_ref, lq2_ref, lk2_ref, g_ref, o_ref,
                 s_refs, p_refs, *, lambda_init):
    lam = (jnp.exp(jnp.sum(lq1_ref[...] * lk1_ref[...], axis=-1, keepdims=True))
           - jnp.exp(jnp.sum(lq2_ref[...] * lk2_ref[...], axis=-1, keepdims=True)) + lambda_init)
    problems = [_diff_problem(qt_ref.at[_tile_rows(u)], k_ref.at[u], vt_ref.at[_tile_rows(u)], causal_ref, lam,
                              g_ref, o_ref.at[u], lambda_init) for u in range(ATT_SUB)]
    _pipelined_attention(k_ref.shape[1] // ATT_TILE, problems, s_refs, p_refs)


def _diff_attention(kn, qvt, b, s, lq1, lk1, lq2, lk2, subln_g, lambda_init, n_heads):
    t = ATT_TILE
    vec = lambda a: a.reshape(1, -1).astype(F32)
    small = lambda n: pl.BlockSpec((1, n), lambda b_, h: (0, 0))
    return pl.pallas_call(
        functools.partial(_diff_kernel, lambda_init=lambda_init),
        grid=(b, n_heads // ATT_SUB),
        in_specs=[_t_spec(s, 0), _k_spec(s, 0), _t_spec(s, n_heads), pl.BlockSpec((t, t), lambda b_, h: (0, 0)),
                  small(HEAD_DIM), small(HEAD_DIM), small(HEAD_DIM), small(HEAD_DIM), small(LANES)],
        out_specs=_k_spec(s, 0),
        out_shape=jax.ShapeDtypeStruct((n_heads, b * s, LANES), BF16),
        scratch_shapes=_attn_scratch(s),
        compiler_params=_cparams(("parallel", "parallel")),
        name="diff_attention",
    )(qvt, kn, qvt, _causal_table_t(t), vec(lq1), vec(lk1), vec(lq2), vec(lk2), vec(subln_g))


def _mixer_out_ffn_kernel(*refs, n_acts, chunk, final_norm):
    h_ref = refs[0]
    act_refs = refs[1:1 + n_acts]
    wo_ref, g_ref, wg_ref, wu_ref, wd_ref, gf_ref, o_ref = refs[1 + n_acts:]
    act = jnp.concatenate([a_ref[c] for a_ref in act_refs for c in range(a_ref.shape[0])], axis=1)
    x = h_ref[...] + jnp.dot(act, wo_ref[...], preferred_element_type=F32)
    ms = jnp.mean(x * x, axis=-1, keepdims=True)
    xn = (x * lax.rsqrt(ms + NORM_EPS) * g_ref[...]).astype(BF16)
    acc = x
    for c in range(wg_ref.shape[1] // chunk):
        sl = slice(c * chunk, (c + 1) * chunk)
        gate = jnp.dot(xn, wg_ref[:, sl], preferred_element_type=F32)
        up = jnp.dot(xn, wu_ref[:, sl], preferred_element_type=F32)
        mid = (gate * jax.nn.sigmoid(gate) * up).astype(BF16)
        acc = acc + jnp.dot(mid, wd_ref[sl, :], preferred_element_type=F32)
    if final_norm:
        ms = jnp.mean(acc * acc, axis=-1, keepdims=True)
        acc = acc * lax.rsqrt(ms + NORM_EPS) * gf_ref[...]
    o_ref[...] = acc


def _mixer_out_ffn(h, acts, w_out, g, layer, wg, wu, wd, g_final, final_norm):
    t, d = h.shape
    ff = wg.shape[2]
    tm = ROW_TILE
    row = lambda i: (i, 0)
    fixed = lambda i: (0, 0)
    of_layer = lambda i: (layer, 0, 0)
    once = dict(pipeline_mode=pl.Buffered(1))
    in_specs = [pl.BlockSpec((tm, d), row)]
    in_specs += [pl.BlockSpec((a.shape[0], tm, LANES), lambda i: (0, i, 0)) for a in acts]
    in_specs += [pl.BlockSpec(w_out.shape, fixed, **once), pl.BlockSpec((1, d), fixed),
                 pl.BlockSpec((None, d, ff), of_layer, **once), pl.BlockSpec((None, d, ff), of_layer, **once),
                 pl.BlockSpec((None, ff, d), of_layer, **once), pl.BlockSpec((1, d), fixed)]
    args = [h, *acts, w_out, g.reshape(1, d), wg, wu, wd, g_final.reshape(1, d)]
    return pl.pallas_call(
        functools.partial(_mixer_out_ffn_kernel, n_acts=len(acts), chunk=256, final_norm=final_norm),
        grid=(t // tm,),
        in_specs=in_specs,
        out_specs=pl.BlockSpec((tm, d), row),
        out_shape=jax.ShapeDtypeStruct((t, d), F32),
        compiler_params=_cparams(("parallel",)),
        name="mixer_out_ffn",
    )(*args)


def _cast_kernel(*refs):
    n = len(refs) // 2
    for src_ref, dst_ref in zip(refs[:n], refs[n:]):
        dst_ref[...] = src_ref[...].astype(BF16)


def _cast_bf16(arrays, n_chunks):
    specs = [pl.BlockSpec((1, a.shape[1] // n_chunks, a.shape[2]), lambda l, c: (l, c, 0)) for a in arrays]
    return pl.pallas_call(
        _cast_kernel,
        grid=(arrays[0].shape[0], n_chunks),
        in_specs=specs,
        out_specs=specs,
        out_shape=[jax.ShapeDtypeStruct(a.shape, BF16) for a in arrays],
        compiler_params=_cparams(("parallel", "parallel")),
        name="cast_weights",
    )(*arrays)


def _lambda_init(layer_idx):
    return 0.8 - 0.6 * math.exp(-0.3 * layer_idx)


def _split_weights(w_in, q_cols, k_cols, v_cols):
    cat = lambda cols: jnp.concatenate([w_in[:, a:b] for a, b in cols], axis=1)
    groups = [(cat([c]) * (ATTN_SCALE * LOG2E) if is_q else cat([c])) for c, is_q in
              sorted([(c, True) for c in q_cols] + [(c, False) for c in v_cols])]
    return cat(k_cols).astype(BF16), jnp.concatenate(groups, axis=1).T.astype(BF16)


def kernel(x, positions, ab_norm_g, ab_w_in, ab_w_out, diff_norm_g, diff_w_in, diff_w_out, diff_lambda_q1, diff_lambda_k1, diff_lambda_q2, diff_lambda_k2, diff_subln_g, ffn_norm_g, ffn_w_gate, ffn_w_up, ffn_w_down, final_norm_g):
    b, s, d = x.shape
    t = b * s
    n_pairs = d // (4 * HEAD_DIM)
    n_diff = d // (2 * HEAD_DIM)
    wa = n_pairs * LANES
    tabs = _rope_tables(positions)
    wg, wu, wd = _cast_bf16([ffn_w_gate, ffn_w_up, ffn_w_down], n_chunks=4)
    h = x.reshape(t, d)

    wk, wt = _split_weights(ab_w_in[0], q_cols=[(0, wa), (3 * wa, 4 * wa)],
                            k_cols=[(wa, 2 * wa), (4 * wa, 5 * wa)], v_cols=[(2 * wa, 3 * wa), (5 * wa, 6 * wa)])
    q_tiles0 = list(range(n_pairs)) + list(range(2 * n_pairs, 3 * n_pairs))
    kn, qvt = _norm_proj(h, ab_norm_g[0], wk, wt, tabs, q_tiles0)
    oa = _dilated_attention(kn, qvt, b, s, 0, 0, n_pairs, n_pairs)
    ob = _moba_attention(kn, qvt, b, s, 2 * n_pairs, n_pairs, 3 * n_pairs, n_pairs)
    h = _mixer_out_ffn(h, [oa, ob], ab_w_out[0].astype(BF16), ffn_norm_g[0], 0, wg, wu, wd, final_norm_g, False)

    wk, wt = _split_weights(diff_w_in[0], q_cols=[(0, d)], k_cols=[(d, 2 * d)], v_cols=[(2 * d, 3 * d)])
    kn, qvt = _norm_proj(h, diff_norm_g[0], wk, wt, tabs, list(range(n_diff)))
    od = _diff_attention(kn, qvt, b, s, diff_lambda_q1[0], diff_lambda_k1[0],
                         diff_lambda_q2[0], diff_lambda_k2[0], diff_subln_g[0], _lambda_init(1), n_diff)
    h = _mixer_out_ffn(h, [od], diff_w_out[0].astype(BF16), ffn_norm_g[1], 1, wg, wu, wd, final_norm_g, True)
    return h.reshape(b, s, d)
```

```python
import collections
import functools
import math

import numpy as np
import jax
import jax.numpy as jnp
from jax import lax
from jax.experimental import pallas as pl
from jax.experimental.pallas import tpu as pltpu

D_MODEL = 1024
HEAD_DIM = 64
LANES = 128
SUBLANES = 8
ROPE_DIM = HEAD_DIM // 4
ROPE_HALF = ROPE_DIM // 2
ROPE_THETA = 500000.0
D_FF = 2816
NORM_EPS = 1e-5
ATTN_SCALE = HEAD_DIM ** -0.5
LOG2E = math.log2(math.e)
NEG = -1e30
DILATED_CONFIGS = ((128, 1), (512, 4), (2048, 16))
MOBA_BLOCK = 256
MOBA_TOPK = 3

ATT_TILE = 256
ATT_SUB = 2
ROW_TILE = 1024
PROJ_ROW_TILE = 1024
VMEM_LIMIT = 56 * 1024 * 1024

F32 = jnp.float32
BF16 = jnp.bfloat16
_NT = (((1,), (1,)), ((), ()))

assert ROPE_HALF == SUBLANES


def _cparams(sem):
    return pltpu.CompilerParams(dimension_semantics=sem, vmem_limit_bytes=VMEM_LIMIT)


def _rope_table_kernel(pos_ref, invf_ref, c_ref, s1_ref, s2_ref, ct_ref, st_ref):
    ang = invf_ref[...] * pos_ref[...].astype(F32)
    cos, sin = jnp.cos(ang), jnp.sin(ang)
    ct_ref[...] = cos
    st_ref[...] = sin
    tm = ang.shape[1]
    one = jnp.ones((HEAD_DIM - ROPE_DIM, tm), F32)
    zero = jnp.zeros((ROPE_HALF, tm), F32)
    zero_rest = jnp.zeros((HEAD_DIM - ROPE_DIM, tm), F32)
    c_ref[...] = jnp.concatenate([cos, cos, one] * 2, axis=0).T
    s1_ref[...] = jnp.concatenate([-sin, zero, zero_rest] * 2, axis=0).T
    s2_ref[...] = jnp.concatenate([zero, sin, zero_rest] * 2, axis=0).T


def _rope_tables(positions):
    t = positions.size
    tm = 1024
    inv_freq = ROPE_THETA ** (-jnp.arange(0, ROPE_DIM, 2, dtype=F32) / ROPE_DIM)
    tab = jax.ShapeDtypeStruct((t, LANES), F32)
    tab_t = jax.ShapeDtypeStruct((ROPE_HALF, t), F32)
    row = lambda i: (i, 0)
    col = lambda i: (0, i)
    return pl.pallas_call(
        _rope_table_kernel,
        grid=(t // tm,),
        in_specs=[pl.BlockSpec((1, tm), col), pl.BlockSpec((ROPE_HALF, 1), lambda i: (0, 0))],
        out_specs=[pl.BlockSpec((tm, LANES), row)] * 3 + [pl.BlockSpec((ROPE_HALF, tm), col)] * 2,
        out_shape=[tab] * 3 + [tab_t] * 2,
        compiler_params=_cparams(("parallel",)),
        name="rope_tables",
    )(positions.reshape(1, t), inv_freq.reshape(ROPE_HALF, 1))


def _norm_proj_kernel(x_ref, g_ref, wk_ref, wt_ref, c_ref, s1_ref, s2_ref, ct_ref, st_ref,
                      k_ref, qvt_ref, acc_ref, *, q_tiles):
    x = x_ref[...]
    ms = jnp.mean(x * x, axis=-1, keepdims=True)
    xn = (x * lax.rsqrt(ms + NORM_EPS) * g_ref[...]).astype(BF16)

    c, s1, s2 = c_ref[...], s1_ref[...], s2_ref[...]
    chunk = 512
    for n in range(wk_ref.shape[1] // chunk):
        acc = jnp.dot(xn, wk_ref[:, n * chunk:(n + 1) * chunk], preferred_element_type=F32)
        for t in range(chunk // LANES):
            seg = acc[:, t * LANES:(t + 1) * LANES]
            seg = seg * c + pltpu.roll(seg, LANES - ROPE_HALF, 1) * s1 + pltpu.roll(seg, ROPE_HALF, 1) * s2
            k_ref[n * (chunk // LANES) + t] = seg.astype(BF16)

    acc_ref[...] = lax.dot_general(wt_ref[...], xn, _NT, preferred_element_type=F32)
    ct, st = ct_ref[...], st_ref[...]
    for r in range(qvt_ref.shape[0] // LANES):
        if r not in q_tiles:
            qvt_ref[r * LANES:(r + 1) * LANES, :] = acc_ref[r * LANES:(r + 1) * LANES, :].astype(BF16)
            continue
        for r0 in range(r * LANES, (r + 1) * LANES, HEAD_DIM):
            x1 = acc_ref[r0:r0 + ROPE_HALF, :]
            x2 = acc_ref[r0 + ROPE_HALF:r0 + ROPE_DIM, :]
            rot = jnp.concatenate([x1 * ct - x2 * st, x2 * ct + x1 * st], axis=0)
            qvt_ref[r0:r0 + ROPE_DIM, :] = rot.astype(BF16)
            qvt_ref[r0 + ROPE_DIM:r0 + HEAD_DIM, :] = acc_ref[r0 + ROPE_DIM:r0 + HEAD_DIM, :].astype(BF16)


def _norm_proj(x, g, wk, wt, tabs, q_tiles):
    t, d = x.shape
    nk, nt = wk.shape[1], wt.shape[0]
    tm = PROJ_ROW_TILE
    row = lambda i: (i, 0)
    col = lambda i: (0, i)
    fixed = lambda i: (0, 0)
    once = dict(pipeline_mode=pl.Buffered(1))
    return pl.pallas_call(
        functools.partial(_norm_proj_kernel, q_tiles=frozenset(q_tiles)),
        grid=(t // tm,),
        in_specs=[pl.BlockSpec((tm, d), row), pl.BlockSpec((1, d), fixed),
                  pl.BlockSpec((d, nk), fixed, **once), pl.BlockSpec((nt, d), fixed, **once),
                  pl.BlockSpec((tm, LANES), row), pl.BlockSpec((tm, LANES), row), pl.BlockSpec((tm, LANES), row),
                  pl.BlockSpec((ROPE_HALF, tm), col), pl.BlockSpec((ROPE_HALF, tm), col)],
        out_specs=[pl.BlockSpec((nk // LANES, tm, LANES), lambda i: (0, i, 0)), pl.BlockSpec((nt, tm), col)],
        out_shape=[jax.ShapeDtypeStruct((nk // LANES, t, LANES), BF16), jax.ShapeDtypeStruct((nt, t), BF16)],
        scratch_shapes=[pltpu.VMEM((nt, tm), F32)],
        compiler_params=_cparams(("parallel",)),
        name="norm_proj_rope",
    )(x, g.reshape(1, d), wk, wt, *tabs)


def _row_masks(shape):
    row = lax.broadcasted_iota(jnp.int32, shape, 0)
    return row < HEAD_DIM, row >= HEAD_DIM


def _causal_table_t(t):
    d = np.arange(t)[None, :] - np.arange(t)[:, None]
    return jnp.asarray(np.where(d >= 0, 0.0, NEG), F32)


def _logits_stage(qt, k_of, bias_of, n_blocks, s_ref):
    t = ATT_TILE
    m = None
    for j in range(n_blocks):
        s = jnp.dot(k_of(j), qt, preferred_element_type=F32)
        bias = bias_of(j)
        if bias is not None:
            s = s + bias
        s_ref[j * t:(j + 1) * t, :] = s
        mj = jnp.max(s, axis=0, keepdims=True)
        m = mj if m is None else jnp.maximum(m, mj)
    return m


def _probs_stage(m, n_blocks, s_ref, p_ref, want_sum):
    t = ATT_TILE
    l = None
    for j in range(n_blocks):
        p = jnp.exp2(s_ref[j * t:(j + 1) * t, :] - m)
        p_ref[j * t:(j + 1) * t, :] = p.astype(BF16)
        if want_sum:
            lj = jnp.sum(p, axis=0, keepdims=True)
            l = lj if l is None else l + lj
    return l


def _values_stage(l, vt_of, n_blocks, p_ref, ones_row):
    n = n_blocks * ATT_TILE
    acc = jnp.dot(vt_of(n), p_ref[:n, :], preferred_element_type=F32)
    if l is None:
        l = acc[ones_row:ones_row + 1, :]
    return acc * (1.0 / l)


_Problem = collections.namedtuple(
    "_Problem", "make_query k_of_head bias_of_tile vt_of_head ones_rows finish_tile")


def _pipelined_attention(n_tiles, problems, s_refs, p_refs):
    items = [(pr, i, h) for i in reversed(range(n_tiles)) for pr in problems for h in range(2)]
    n = len(items)
    ms, ls, outs = {}, {}, {}
    for step in range(n + 2):
        if 1 <= step <= n:
            c = step - 1
            pr, i, h = items[c]
            ls[c] = _probs_stage(ms.pop(c), i + 1, s_refs.at[c % 2], p_refs.at[c % 2], pr.ones_rows[h] is None)
        if step < n:
            pr, i, h = items[step]
            ms[step] = _logits_stage(pr.make_query(i, h), pr.k_of_head(h), pr.bias_of_tile(i), i + 1,
                                     s_refs.at[step % 2])
        if 2 <= step <= n + 1:
            c = step - 2
            pr, i, h = items[c]
            outs[c] = _values_stage(ls.pop(c), pr.vt_of_head(h), i + 1, p_refs.at[c % 2], pr.ones_rows[h])
            if h == 1:
                pr.finish_tile(i, outs.pop(c - 1), outs.pop(c))


def _values_with_ones(vt_ref, vta_ref, vtb_ref):
    ones = jnp.ones((HEAD_DIM, vt_ref.shape[1]), BF16)
    vta_ref[:HEAD_DIM, :] = vt_ref[:HEAD_DIM, :]
    vta_ref[HEAD_DIM:, :] = ones
    vtb_ref[:HEAD_DIM, :] = ones
    vtb_ref[HEAD_DIM:, :] = vt_ref[HEAD_DIM:, :]
    vt_of = (lambda n: vta_ref[:, :n], lambda n: vtb_ref[:, :n])
    return (lambda h: vt_of[h]), (HEAD_DIM, 0)


def _key_blocks(ref):
    t = ATT_TILE
    return lambda j: ref[j * t:(j + 1) * t, :]


def _tile_rows(u):
    return pl.ds(u * LANES, LANES)


def _attn_scratch(s):
    return [pltpu.VMEM((2, s, ATT_TILE), F32), pltpu.VMEM((2, s, ATT_TILE), BF16)]


def _k_spec(s, tile0):
    assert tile0 % ATT_SUB == 0
    return pl.BlockSpec((ATT_SUB, s, LANES), lambda b, p: (tile0 // ATT_SUB + p, b, 0))


def _t_spec(s, tile0):
    assert tile0 % ATT_SUB == 0
    return pl.BlockSpec((ATT_SUB * LANES, s), lambda b, p: (tile0 // ATT_SUB + p, b))


def _dilated_bias_tables_t(t):
    windows = sorted(w for w, _ in DILATED_CONFIGS)
    far = windows[-2] // t + 1
    tabs = []
    for delta in range(far + 1):
        d = delta * t + np.arange(t)[None, :] - np.arange(t)[:, None]
        mult = np.zeros((t, t), np.float64)
        for window, dil in DILATED_CONFIGS:
            mult += (d >= 0) & (d % dil == 0) & (d <= window)
        tabs.append(np.where(mult > 0, np.log2(np.maximum(mult, 1.0)), NEG))
    return jnp.asarray(np.stack(tabs), F32)


def _dilated_problem(qt_ref, k_ref, vt_ref, bias_ref, o_ref, vta_ref, vtb_ref):
    t = ATT_TILE
    n_tab = bias_ref.shape[0]
    k_of = _key_blocks(k_ref)
    vt_of_head, ones_rows = _values_with_ones(vt_ref, vta_ref, vtb_ref)

    def make_query(i, h):
        qt = qt_ref[:, i * t:(i + 1) * t].astype(F32)
        return jnp.where(_row_masks(qt.shape)[h], qt, 0.0).astype(BF16)

    def finish_tile(i, out_lo, out_hi):
        lo, _ = _row_masks(out_lo.shape)
        o_ref[i * t:(i + 1) * t, :] = jnp.where(lo, out_lo, out_hi).T.astype(BF16)

    return _Problem(make_query, lambda h: k_of, lambda i: (lambda j: bias_ref[min(i - j, n_tab - 1)]),
                    vt_of_head, ones_rows, finish_tile)


def _dilated_kernel(qt_ref, k_ref, vt_ref, bias_ref, o_ref, vta_ref, vtb_ref, s_refs, p_refs):
    problems = [_dilated_problem(qt_ref.at[_tile_rows(u)], k_ref.at[u], vt_ref.at[_tile_rows(u)], bias_ref,
                                 o_ref.at[u], vta_ref.at[u], vtb_ref.at[u]) for u in range(ATT_SUB)]
    _pipelined_attention(k_ref.shape[1] // ATT_TILE, problems, s_refs, p_refs)


def _ones_value_scratch(s):
    return [pltpu.VMEM((ATT_SUB, LANES, s), BF16), pltpu.VMEM((ATT_SUB, LANES, s), BF16)]


def _dilated_attention(kn, qvt, b, s, q_tile0, k_tile0, v_tile0, n_pairs):
    bias = _dilated_bias_tables_t(ATT_TILE)
    return pl.pallas_call(
        _dilated_kernel,
        grid=(b, n_pairs // ATT_SUB),
        in_specs=[_t_spec(s, q_tile0), _k_spec(s, k_tile0), _t_spec(s, v_tile0),
                  pl.BlockSpec(bias.shape, lambda b_, p: (0, 0, 0))],
        out_specs=_k_spec(s, 0),
        out_shape=jax.ShapeDtypeStruct((n_pairs, b * s, LANES), BF16),
        scratch_shapes=_ones_value_scratch(s) + _attn_scratch(s),
        compiler_params=_cparams(("parallel", "parallel")),
        name="dilated_attention",
    )(qvt, kn, qvt, bias)


def _moba_block_onehots(s):
    blk = np.arange(s) // MOBA_BLOCK
    onehot = np.zeros((2, s, LANES), np.float32)
    onehot[0, np.arange(s), HEAD_DIM + blk] = 1.0
    onehot[1, np.arange(s), blk] = 1.0
    mask = np.zeros((2, 1, LANES), np.float32)
    mask[0, 0, :HEAD_DIM] = 1.0
    mask[1, 0, HEAD_DIM:] = 1.0
    return jnp.asarray(onehot, BF16), jnp.asarray(mask, BF16)


def _moba_problem(qt_ref, k_ref, vt_ref, causal_ref, onehot_ref, mask_ref, o_ref, ka_ref, kb_ref, vta_ref, vtb_ref):
    t = MOBA_BLOCK
    vt_of_head, ones_rows = _values_with_ones(vt_ref, vta_ref, vtb_ref)
    s_len = k_ref.shape[0]
    nblk = s_len // t

    ka_ref[...] = k_ref[...] * mask_ref[0] + onehot_ref[0]
    kb_ref[...] = k_ref[...] * mask_ref[1] + onehot_ref[1]
    means = jnp.concatenate(
        [jnp.mean(k_ref[n * t:(n + 1) * t, :].astype(F32), axis=0, keepdims=True) for n in range(nblk)], axis=0)
    lane_m = lax.broadcasted_iota(jnp.int32, means.shape, 1)
    means2 = jnp.concatenate([jnp.where(lane_m < HEAD_DIM, means, 0.0),
                              jnp.where(lane_m >= HEAD_DIM, means, 0.0)], axis=0)

    hi = means2.astype(BF16)
    rest = means2 - hi.astype(F32)
    mid = rest.astype(BF16)
    lo = (rest - mid.astype(F32)).astype(BF16)
    means_split = jnp.concatenate([hi, mid, lo], axis=0)
    gates = {}

    def gate_scores(i):
        if i not in gates:
            g3 = jnp.dot(means_split, qt_ref[:, i * t:(i + 1) * t], preferred_element_type=F32)
            gates[i] = g3[:2 * nblk] + g3[2 * nblk:4 * nblk] + g3[4 * nblk:]
        return gates[i]

    zeros = jnp.zeros((HEAD_DIM - nblk, t), F32)
    k_of = (_key_blocks(ka_ref), _key_blocks(kb_ref))

    def make_query(i, h):
        qt = qt_ref[:, i * t:(i + 1) * t].astype(F32)
        if i <= MOBA_TOPK:
            return jnp.where(_row_masks(qt.shape)[h], qt, 0.0).astype(BF16)
        g = gate_scores(i)[h * nblk:(h + 1) * nblk]
        row = lax.broadcasted_iota(jnp.int32, g.shape, 0)
        rank = jnp.zeros(g.shape, F32)
        for kk in range(i):
            gk = g[kk:kk + 1, :]
            beats = (gk > g) | ((gk == g) & (row > kk))
            rank = rank + jnp.where(beats, 1.0, 0.0)
        drop = jnp.where((row < i) & (rank >= MOBA_TOPK), NEG, 0.0)
        parts = [qt[:HEAD_DIM], drop, zeros] if h == 0 else [drop, zeros, qt[HEAD_DIM:]]
        return jnp.concatenate(parts, axis=0).astype(BF16)

    def finish_tile(i, out_lo, out_hi):
        lo, _ = _row_masks(out_lo.shape)
        o_ref[i * t:(i + 1) * t, :] = jnp.where(lo, out_lo, out_hi).T.astype(BF16)

    return _Problem(make_query, lambda h: k_of[h], lambda i: (lambda j: causal_ref[...] if j == i else None),
                    vt_of_head, ones_rows, finish_tile)


def _moba_kernel(qt_ref, k_ref, vt_ref, causal_ref, onehot_ref, mask_ref, o_ref, ka_ref, kb_ref, vta_ref, vtb_ref,
                 s_refs, p_refs):
    problems = [_moba_problem(qt_ref.at[_tile_rows(u)], k_ref.at[u], vt_ref.at[_tile_rows(u)], causal_ref,
                              onehot_ref, mask_ref, o_ref.at[u], ka_ref.at[u], kb_ref.at[u], vta_ref.at[u],
                              vtb_ref.at[u]) for u in range(ATT_SUB)]
    _pipelined_attention(k_ref.shape[1] // ATT_TILE, problems, s_refs, p_refs)


def _moba_attention(kn, qvt, b, s, q_tile0, k_tile0, v_tile0, n_pairs):
    t = MOBA_BLOCK
    assert t == ATT_TILE and s // t <= SUBLANES
    onehot, mask = _moba_block_onehots(s)
    fixed3 = lambda b_, p: (0, 0, 0)
    return pl.pallas_call(
        _moba_kernel,
        grid=(b, n_pairs // ATT_SUB),
        in_specs=[_t_spec(s, q_tile0), _k_spec(s, k_tile0), _t_spec(s, v_tile0),
                  pl.BlockSpec((t, t), lambda b_, p: (0, 0)),
                  pl.BlockSpec(onehot.shape, fixed3), pl.BlockSpec(mask.shape, fixed3)],
        out_specs=_k_spec(s, 0),
        out_shape=jax.ShapeDtypeStruct((n_pairs, b * s, LANES), BF16),
        scratch_shapes=[pltpu.VMEM((ATT_SUB, s, LANES), BF16), pltpu.VMEM((ATT_SUB, s, LANES), BF16)]
        + _ones_value_scratch(s) + _attn_scratch(s),
        compiler_params=_cparams(("parallel", "parallel")),
        name="moba_attention",
    )(qvt, kn, qvt, _causal_table_t(t), onehot, mask)


def _diff_problem(qt_ref, k_ref, vt_ref, causal_ref, lam, g_ref, o_ref, lambda_init):
    t = ATT_TILE
    k_of = _key_blocks(k_ref)

    def make_query(i, h):
        qt = qt_ref[:, i * t:(i + 1) * t].astype(F32)
        return jnp.where(_row_masks(qt.shape)[h], qt, 0.0).astype(BF16)

    def finish_tile(i, out1, out2):
        o = out1 - lam * out2
        y = o * lax.rsqrt(jnp.mean(o * o, axis=0, keepdims=True) + NORM_EPS)
        o_ref[i * t:(i + 1) * t, :] = ((y.T * g_ref[...]) * (1.0 - lambda_init)).astype(BF16)

    return _Problem(make_query, lambda h: k_of, lambda i: (lambda j: causal_ref[...] if j == i else None),
                    lambda h: (lambda n: vt_ref[:, :n]), (None, None), finish_tile)


def _diff_kernel(qt_ref, k_ref, vt_ref, causal_ref, lq1_ref, lk1_ref, lq2_ref, lk2_ref, g_ref, o_ref,
                 s_refs, p_refs, *, lambda_init):
    lam = (jnp.exp(jnp.sum(lq1_ref[...] * lk1_ref[...], axis=-1, keepdims=True))
           - jnp.exp(jnp.sum(lq2_ref[...] * lk2_ref[...], axis=-1, keepdims=True)) + lambda_init)
    problems = [_diff_problem(qt_ref.at[_tile_rows(u)], k_ref.at[u], vt_ref.at[_tile_rows(u)], causal_ref, lam,
                              g_ref, o_ref.at[u], lambda_init) for u in range(ATT_SUB)]
    _pipelined_attention(k_ref.shape[1] // ATT_TILE, problems, s_refs, p_refs)


def _diff_attention(kn, qvt, b, s, lq1, lk1, lq2, lk2, subln_g, lambda_init, n_heads):
    t = ATT_TILE
    vec = lambda a: a.reshape(1, -1).astype(F32)
    small = lambda n: pl.BlockSpec((1, n), lambda b_, h: (0, 0))
    return pl.pallas_call(
        functools.partial(_diff_kernel, lambda_init=lambda_init),
        grid=(b, n_heads // ATT_SUB),
        in_specs=[_t_spec(s, 0), _k_spec(s, 0), _t_spec(s, n_heads), pl.BlockSpec((t, t), lambda b_, h: (0, 0)),
                  small(HEAD_DIM), small(HEAD_DIM), small(HEAD_DIM), small(HEAD_DIM), small(LANES)],
        out_specs=_k_spec(s, 0),
        out_shape=jax.ShapeDtypeStruct((n_heads, b * s, LANES), BF16),
        scratch_shapes=_attn_scratch(s),
        compiler_params=_cparams(("parallel", "parallel")),
        name="diff_attention",
    )(qvt, kn, qvt, _causal_table_t(t), vec(lq1), vec(lk1), vec(lq2), vec(lk2), vec(subln_g))


def _mixer_out_ffn_kernel(*refs, n_acts, chunk, final_norm):
    h_ref = refs[0]
    act_refs = refs[1:1 + n_acts]
    wo_ref, g_ref, wg_ref, wu_ref, wd_ref, gf_ref, o_ref = refs[1 + n_acts:]
    act = jnp.concatenate([a_ref[c] for a_ref in act_refs for c in range(a_ref.shape[0])], axis=1)
    x = h_ref[...] + jnp.dot(act, wo_ref[...], preferred_element_type=F32)
    ms = jnp.mean(x * x, axis=-1, keepdims=True)
    xn = (x * lax.rsqrt(ms + NORM_EPS) * g_ref[...]).astype(BF16)
    acc = x
    for c in range(wg_ref.shape[1] // chunk):
        sl = slice(c * chunk, (c + 1) * chunk)
        gate = jnp.dot(xn, wg_ref[:, sl], preferred_element_type=F32)
        up = jnp.dot(xn, wu_ref[:, sl], preferred_element_type=F32)
        mid = (gate * jax.nn.sigmoid(gate) * up).astype(BF16)
        acc = acc + jnp.dot(mid, wd_ref[sl, :], preferred_element_type=F32)
    if final_norm:
        ms = jnp.mean(acc * acc, axis=-1, keepdims=True)
        acc = acc * lax.rsqrt(ms + NORM_EPS) * gf_ref[...]
    o_ref[...] = acc


def _mixer_out_ffn(h, acts, w_out, g, layer, wg, wu, wd, g_final, final_norm):
    t, d = h.shape
    ff = wg.shape[2]
    tm = ROW_TILE
    row = lambda i: (i, 0)
    fixed = lambda i: (0, 0)
    of_layer = lambda i: (layer, 0, 0)
    once = dict(pipeline_mode=pl.Buffered(1))
    in_specs = [pl.BlockSpec((tm, d), row)]
    in_specs += [pl.BlockSpec((a.shape[0], tm, LANES), lambda i: (0, i, 0)) for a in acts]
    in_specs += [pl.BlockSpec(w_out.shape, fixed, **once), pl.BlockSpec((1, d), fixed),
                 pl.BlockSpec((None, d, ff), of_layer, **once), pl.BlockSpec((None, d, ff), of_layer, **once),
                 pl.BlockSpec((None, ff, d), of_layer, **once), pl.BlockSpec((1, d), fixed)]
    args = [h, *acts, w_out, g.reshape(1, d), wg, wu, wd, g_final.reshape(1, d)]
    return pl.pallas_call(
        functools.partial(_mixer_out_ffn_kernel, n_acts=len(acts), chunk=256, final_norm=final_norm),
        grid=(t // tm,),
        in_specs=in_specs,
        out_specs=pl.BlockSpec((tm, d), row),
        out_shape=jax.ShapeDtypeStruct((t, d), F32),
        compiler_params=_cparams(("parallel",)),
        name="mixer_out_ffn",
    )(*args)


def _cast_kernel(*refs):
    n = len(refs) // 2
    for src_ref, dst_ref in zip(refs[:n], refs[n:]):
        dst_ref[...] = src_ref[...].astype(BF16)


def _cast_bf16(arrays, n_chunks):
    specs = [pl.BlockSpec((1, a.shape[1] // n_chunks, a.shape[2]), lambda l, c: (l, c, 0)) for a in arrays]
    return pl.pallas_call(
        _cast_kernel,
        grid=(arrays[0].shape[0], n_chunks),
        in_specs=specs,
        out_specs=specs,
        out_shape=[jax.ShapeDtypeStruct(a.shape, BF16) for a in arrays],
        compiler_params=_cparams(("parallel", "parallel")),
        name="cast_weights",
    )(*arrays)


def _lambda_init(layer_idx):
    return 0.8 - 0.6 * math.exp(-0.3 * layer_idx)


def _split_weights(w_in, q_cols, k_cols, v_cols):
    cat = lambda cols: jnp.concatenate([w_in[:, a:b] for a, b in cols], axis=1)
    groups = [(cat([c]) * (ATTN_SCALE * LOG2E) if is_q else cat([c])) for c, is_q in
              sorted([(c, True) for c in q_cols] + [(c, False) for c in v_cols])]
    return cat(k_cols).astype(BF16), jnp.concatenate(groups, axis=1).T.astype(BF16)


def kernel(x, positions, ab_norm_g, ab_w_in, ab_w_out, diff_norm_g, diff_w_in, diff_w_out, diff_lambda_q1, diff_lambda_k1, diff_lambda_q2, diff_lambda_k2, diff_subln_g, ffn_norm_g, ffn_w_gate, ffn_w_up, ffn_w_down, final_norm_g):
    b, s, d = x.shape
    t = b * s
    n_pairs = d // (4 * HEAD_DIM)
    n_diff = d // (2 * HEAD_DIM)
    wa = n_pairs * LANES
    tabs = _rope_tables(positions)
    wg, wu, wd = _cast_bf16([ffn_w_gate, ffn_w_up, ffn_w_down], n_chunks=4)
    h = x.reshape(t, d)

    wk, wt = _split_weights(ab_w_in[0], q_cols=[(0, wa), (3 * wa, 4 * wa)],
                            k_cols=[(wa, 2 * wa), (4 * wa, 5 * wa)], v_cols=[(2 * wa, 3 * wa), (5 * wa, 6 * wa)])
    q_tiles0 = list(range(n_pairs)) + list(range(2 * n_pairs, 3 * n_pairs))
    kn, qvt = _norm_proj(h, ab_norm_g[0], wk, wt, tabs, q_tiles0)
    oa = _dilated_attention(kn, qvt, b, s, 0, 0, n_pairs, n_pairs)
    ob = _moba_attention(kn, qvt, b, s, 2 * n_pairs, n_pairs, 3 * n_pairs, n_pairs)
    h = _mixer_out_ffn(h, [oa, ob], ab_w_out[0].astype(BF16), ffn_norm_g[0], 0, wg, wu, wd, final_norm_g, False)

    wk, wt = _split_weights(diff_w_in[0], q_cols=[(0, d)], k_cols=[(d, 2 * d)], v_cols=[(2 * d, 3 * d)])
    kn, qvt = _norm_proj(h, diff_norm_g[0], wk, wt, tabs, list(range(n_diff)))
    od = _diff_attention(kn, qvt, b, s, diff_lambda_q1[0], diff_lambda_k1[0],
                         diff_lambda_q2[0], diff_lambda_k2[0], diff_subln_g[0], _lambda_init(1), n_diff)
    h = _mixer_out_ffn(h, [od], diff_w_out[0].astype(BF16), ffn_norm_g[1], 1, wg, wu, wd, final_norm_g, True)
    return h.reshape(b, s, d)
```

```python
import collections
import functools
import math

import numpy as np
import jax
import jax.numpy as jnp
from jax import lax
from jax.experimental import pallas as pl
from jax.experimental.pallas import tpu as pltpu

D_MODEL = 1024
HEAD_DIM = 64
LANES = 128
SUBLANES = 8
ROPE_DIM = HEAD_DIM // 4
ROPE_HALF = ROPE_DIM // 2
ROPE_THETA = 500000.0
D_FF = 2816
NORM_EPS = 1e-5
ATTN_SCALE = HEAD_DIM ** -0.5
LOG2E = math.log2(math.e)
NEG = -1e30
DILATED_CONFIGS = ((128, 1), (512, 4), (2048, 16))
MOBA_BLOCK = 256
MOBA_TOPK = 3

ATT_TILE = 256
ATT_SUB = 2
SETS = 2
ROW_TILE = 1024
PROJ_ROW_TILE = 1024
VMEM_LIMIT = 56 * 1024 * 1024

F32 = jnp.float32
BF16 = jnp.bfloat16
_NT = (((1,), (1,)), ((), ()))

assert ROPE_HALF == SUBLANES


def _cparams(sem):
    return pltpu.CompilerParams(dimension_semantics=sem, vmem_limit_bytes=VMEM_LIMIT)


def _rope_table_kernel(pos_ref, invf_ref, c_ref, s1_ref, s2_ref, ct_ref, st_ref):
    ang = invf_ref[...] * pos_ref[...].astype(F32)
    cos, sin = jnp.cos(ang), jnp.sin(ang)
    ct_ref[...] = cos
    st_ref[...] = sin
    tm = ang.shape[1]
    one = jnp.ones((HEAD_DIM - ROPE_DIM, tm), F32)
    zero = jnp.zeros((ROPE_HALF, tm), F32)
    zero_rest = jnp.zeros((HEAD_DIM - ROPE_DIM, tm), F32)
    c_ref[...] = jnp.concatenate([cos, cos, one] * 2, axis=0).T
    s1_ref[...] = jnp.concatenate([-sin, zero, zero_rest] * 2, axis=0).T
    s2_ref[...] = jnp.concatenate([zero, sin, zero_rest] * 2, axis=0).T


def _rope_tables(positions):
    t = positions.size
    tm = 1024
    inv_freq = ROPE_THETA ** (-jnp.arange(0, ROPE_DIM, 2, dtype=F32) / ROPE_DIM)
    tab = jax.ShapeDtypeStruct((t, LANES), F32)
    tab_t = jax.ShapeDtypeStruct((ROPE_HALF, t), F32)
    row = lambda i: (i, 0)
    col = lambda i: (0, i)
    return pl.pallas_call(
        _rope_table_kernel,
        grid=(t // tm,),
        in_specs=[pl.BlockSpec((1, tm), col), pl.BlockSpec((ROPE_HALF, 1), lambda i: (0, 0))],
        out_specs=[pl.BlockSpec((tm, LANES), row)] * 3 + [pl.BlockSpec((ROPE_HALF, tm), col)] * 2,
        out_shape=[tab] * 3 + [tab_t] * 2,
        compiler_params=_cparams(("parallel",)),
        name="rope_tables",
    )(positions.reshape(1, t), inv_freq.reshape(ROPE_HALF, 1))


def _norm_proj_kernel(x_ref, g_ref, wk_ref, wt_ref, c_ref, s1_ref, s2_ref, ct_ref, st_ref,
                      k_ref, qvt_ref, acc_ref, *, q_tiles):
    x = x_ref[...]
    ms = jnp.mean(x * x, axis=-1, keepdims=True)
    xn = (x * lax.rsqrt(ms + NORM_EPS) * g_ref[...]).astype(BF16)

    c, s1, s2 = c_ref[...], s1_ref[...], s2_ref[...]
    chunk = 512
    for n in range(wk_ref.shape[1] // chunk):
        acc = jnp.dot(xn, wk_ref[:, n * chunk:(n + 1) * chunk], preferred_element_type=F32)
        for t in range(chunk // LANES):
            seg = acc[:, t * LANES:(t + 1) * LANES]
            seg = seg * c + pltpu.roll(seg, LANES - ROPE_HALF, 1) * s1 + pltpu.roll(seg, ROPE_HALF, 1) * s2
            k_ref[n * (chunk // LANES) + t] = seg.astype(BF16)

    acc_ref[...] = lax.dot_general(wt_ref[...], xn, _NT, preferred_element_type=F32)
    ct, st = ct_ref[...], st_ref[...]
    for r in range(qvt_ref.shape[0] // LANES):
        if r not in q_tiles:
            qvt_ref[r * LANES:(r + 1) * LANES, :] = acc_ref[r * LANES:(r + 1) * LANES, :].astype(BF16)
            continue
        for r0 in range(r * LANES, (r + 1) * LANES, HEAD_DIM):
            x1 = acc_ref[r0:r0 + ROPE_HALF, :]
            x2 = acc_ref[r0 + ROPE_HALF:r0 + ROPE_DIM, :]
            rot = jnp.concatenate([x1 * ct - x2 * st, x2 * ct + x1 * st], axis=0)
            qvt_ref[r0:r0 + ROPE_DIM, :] = rot.astype(BF16)
            qvt_ref[r0 + ROPE_DIM:r0 + HEAD_DIM, :] = acc_ref[r0 + ROPE_DIM:r0 + HEAD_DIM, :].astype(BF16)


def _norm_proj(x, g, wk, wt, tabs, q_tiles):
    t, d = x.shape
    nk, nt = wk.shape[1], wt.shape[0]
    tm = PROJ_ROW_TILE
    row = lambda i: (i, 0)
    col = lambda i: (0, i)
    fixed = lambda i: (0, 0)
    once = dict(pipeline_mode=pl.Buffered(1))
    return pl.pallas_call(
        functools.partial(_norm_proj_kernel, q_tiles=frozenset(q_tiles)),
        grid=(t // tm,),
        in_specs=[pl.BlockSpec((tm, d), row), pl.BlockSpec((1, d), fixed),
                  pl.BlockSpec((d, nk), fixed, **once), pl.BlockSpec((nt, d), fixed, **once),
                  pl.BlockSpec((tm, LANES), row), pl.BlockSpec((tm, LANES), row), pl.BlockSpec((tm, LANES), row),
                  pl.BlockSpec((ROPE_HALF, tm), col), pl.BlockSpec((ROPE_HALF, tm), col)],
        out_specs=[pl.BlockSpec((nk // LANES, tm, LANES), lambda i: (0, i, 0)), pl.BlockSpec((nt, tm), col)],
        out_shape=[jax.ShapeDtypeStruct((nk // LANES, t, LANES), BF16), jax.ShapeDtypeStruct((nt, t), BF16)],
        scratch_shapes=[pltpu.VMEM((nt, tm), F32)],
        compiler_params=_cparams(("parallel",)),
        name="norm_proj_rope",
    )(x, g.reshape(1, d), wk, wt, *tabs)


def _row_masks(shape):
    row = lax.broadcasted_iota(jnp.int32, shape, 0)
    return row < HEAD_DIM, row >= HEAD_DIM


def _causal_table_t(t):
    d = np.arange(t)[None, :] - np.arange(t)[:, None]
    return jnp.asarray(_check_diagonal_table(np.where(d >= 0, 0.0, NEG)), F32)


def _check_diagonal_table(table):
    half = table.shape[0] // 2
    assert (table[half:, :half] == NEG).all()
    return table


def _logits_stage(qt, k_of, bias_of, n_blocks, s_ref):
    t = ATT_TILE
    hb = t // 2
    m = None
    for j in range(n_blocks - 1):
        s = jnp.dot(k_of(j), qt, preferred_element_type=F32)
        bias = bias_of(j)
        if bias is not None:
            s = s + bias
        s_ref[j * t:(j + 1) * t, :] = s
        mj = jnp.max(s, axis=0, keepdims=True)
        m = mj if m is None else jnp.maximum(m, mj)
    j = n_blocks - 1
    s = jnp.dot(k_of(j), qt, preferred_element_type=F32)
    bias = bias_of(j)
    top = s[:hb] + bias[:hb]
    bot = s[hb:, hb:] + bias[hb:, hb:]
    s_ref[j * t:j * t + hb, :] = top
    s_ref[j * t + hb:(j + 1) * t, :] = jnp.concatenate([bot, bot], axis=1)
    m_top = jnp.max(top, axis=0, keepdims=True)
    m_bot = jnp.max(bot, axis=0, keepdims=True)
    mj = jnp.concatenate([m_top[:, :hb], jnp.maximum(m_top[:, hb:], m_bot)], axis=1)
    return mj if m is None else jnp.maximum(m, mj)


def _probs_stage(m, n_blocks, s_ref, p_ref, want_sum):
    t = ATT_TILE
    hb = t // 2
    l = None
    for j in range(n_blocks - 1):
        p = jnp.exp2(s_ref[j * t:(j + 1) * t, :] - m)
        p_ref[j * t:(j + 1) * t, :] = p.astype(BF16)
        if want_sum:
            lj = jnp.sum(p, axis=0, keepdims=True)
            l = lj if l is None else l + lj
    j = n_blocks - 1
    p_top = jnp.exp2(s_ref[j * t:j * t + hb, :] - m)
    p_bot = jnp.exp2(s_ref[j * t + hb:(j + 1) * t, hb:] - m[:, hb:])
    p_ref[j * t:j * t + hb, :] = p_top.astype(BF16)
    p_ref[j * t + hb:(j + 1) * t, :] = jnp.concatenate([jnp.zeros_like(p_bot), p_bot], axis=1).astype(BF16)
    if want_sum:
        l_top = jnp.sum(p_top, axis=0, keepdims=True)
        l_bot = jnp.sum(p_bot, axis=0, keepdims=True)
        lj = jnp.concatenate([l_top[:, :hb], l_top[:, hb:] + l_bot], axis=1)
        l = lj if l is None else l + lj
    return l


def _values_stage(l, vt_of, n_blocks, p_ref, ones_row):
    n = n_blocks * ATT_TILE
    acc = jnp.dot(vt_of(n), p_ref[:n, :], preferred_element_type=F32)
    if l is None:
        l = acc[ones_row:ones_row + 1, :]
    return acc * (1.0 / l)


_Problem = collections.namedtuple(
    "_Problem", "make_query k_of_head bias_of_tile vt_of_head ones_rows finish_tile")


def _pipelined_attention(n_tiles, problems, s_refs, p_refs):
    items = [(pr, i, h) for i in reversed(range(n_tiles)) for pr in problems for h in range(2)]
    n = len(items)
    ms, ls, outs = {}, {}, {}
    for step in range(n + 2):
        if 1 <= step <= n:
            c = step - 1
            pr, i, h = items[c]
            ls[c] = _probs_stage(ms.pop(c), i + 1, s_refs.at[c % SETS], p_refs.at[c % SETS],
                                 pr.ones_rows[h] is None)
        if step < n:
            pr, i, h = items[step]
            ms[step] = _logits_stage(pr.make_query(i, h), pr.k_of_head(h), pr.bias_of_tile(i), i + 1,
                                     s_refs.at[step % SETS])
        if 2 <= step <= n + 1:
            c = step - 2
            pr, i, h = items[c]
            outs[c] = _values_stage(ls.pop(c), pr.vt_of_head(h), i + 1, p_refs.at[c % SETS], pr.ones_rows[h])
            if h == 1:
                pr.finish_tile(i, outs.pop(c - 1), outs.pop(c))


def _values_with_ones(vt_ref, vta_ref, vtb_ref):
    ones = jnp.ones((HEAD_DIM, vt_ref.shape[1]), BF16)
    vta_ref[:HEAD_DIM, :] = vt_ref[:HEAD_DIM, :]
    vta_ref[HEAD_DIM:, :] = ones
    vtb_ref[:HEAD_DIM, :] = ones
    vtb_ref[HEAD_DIM:, :] = vt_ref[HEAD_DIM:, :]
    vt_of = (lambda n: vta_ref[:, :n], lambda n: vtb_ref[:, :n])
    return (lambda h: vt_of[h]), (HEAD_DIM, 0)


def _key_blocks(ref):
    t = ATT_TILE
    return lambda j: ref[j * t:(j + 1) * t, :]


def _tile_rows(u):
    return pl.ds(u * LANES, LANES)


def _attn_scratch(s):
    return [pltpu.VMEM((SETS, s, ATT_TILE), F32), pltpu.VMEM((SETS, s, ATT_TILE), BF16)]


def _k_spec(s, tile0):
    assert tile0 % ATT_SUB == 0
    return pl.BlockSpec((ATT_SUB, s, LANES), lambda b, p: (tile0 // ATT_SUB + p, b, 0))


def _t_spec(s, tile0):
    assert tile0 % ATT_SUB == 0
    return pl.BlockSpec((ATT_SUB * LANES, s), lambda b, p: (tile0 // ATT_SUB + p, b))


def _dilated_bias_tables_t(t):
    windows = sorted(w for w, _ in DILATED_CONFIGS)
    far = windows[-2] // t + 1
    tabs = []
    for delta in range(far + 1):
        d = delta * t + np.arange(t)[None, :] - np.arange(t)[:, None]
        mult = np.zeros((t, t), np.float64)
        for window, dil in DILATED_CONFIGS:
            mult += (d >= 0) & (d % dil == 0) & (d <= window)
        tabs.append(np.where(mult > 0, np.log2(np.maximum(mult, 1.0)), NEG))
    _check_diagonal_table(tabs[0])
    return jnp.asarray(np.stack(tabs), F32)


def _dilated_problem(qt_ref, k_ref, vt_ref, bias_ref, o_ref, vta_ref, vtb_ref):
    t = ATT_TILE
    n_tab = bias_ref.shape[0]
    k_of = _key_blocks(k_ref)
    vt_of_head, ones_rows = _values_with_ones(vt_ref, vta_ref, vtb_ref)

    def make_query(i, h):
        qt = qt_ref[:, i * t:(i + 1) * t].astype(F32)
        return jnp.where(_row_masks(qt.shape)[h], qt, 0.0).astype(BF16)

    def finish_tile(i, out_lo, out_hi):
        lo, _ = _row_masks(out_lo.shape)
        o_ref[i * t:(i + 1) * t, :] = jnp.where(lo, out_lo, out_hi).T.astype(BF16)

    return _Problem(make_query, lambda h: k_of, lambda i: (lambda j: bias_ref[min(i - j, n_tab - 1)]),
                    vt_of_head, ones_rows, finish_tile)


def _dilated_kernel(qt_ref, k_ref, vt_ref, bias_ref, o_ref, vta_ref, vtb_ref, s_refs, p_refs):
    problems = [_dilated_problem(qt_ref.at[_tile_rows(u)], k_ref.at[u], vt_ref.at[_tile_rows(u)], bias_ref,
                                 o_ref.at[u], vta_ref.at[u], vtb_ref.at[u]) for u in range(ATT_SUB)]
    _pipelined_attention(k_ref.shape[1] // ATT_TILE, problems, s_refs, p_refs)


def _ones_value_scratch(s):
    return [pltpu.VMEM((ATT_SUB, LANES, s), BF16), pltpu.VMEM((ATT_SUB, LANES, s), BF16)]


def _dilated_attention(kn, qvt, b, s, q_tile0, k_tile0, v_tile0, n_pairs):
    bias = _dilated_bias_tables_t(ATT_TILE)
    return pl.pallas_call(
        _dilated_kernel,
        grid=(b, n_pairs // ATT_SUB),
        in_specs=[_t_spec(s, q_tile0), _k_spec(s, k_tile0), _t_spec(s, v_tile0),
                  pl.BlockSpec(bias.shape, lambda b_, p: (0, 0, 0))],
        out_specs=_k_spec(s, 0),
        out_shape=jax.ShapeDtypeStruct((n_pairs, b * s, LANES), BF16),
        scratch_shapes=_ones_value_scratch(s) + _attn_scratch(s),
        compiler_params=_cparams(("parallel", "parallel")),
        name="dilated_attention",
    )(qvt, kn, qvt, bias)


def _moba_block_onehots(s):
    blk = np.arange(s) // MOBA_BLOCK
    onehot = np.zeros((2, s, LANES), np.float32)
    onehot[0, np.arange(s), HEAD_DIM + blk] = 1.0
    onehot[1, np.arange(s), blk] = 1.0
    mask = np.zeros((2, 1, LANES), np.float32)
    mask[0, 0, :HEAD_DIM] = 1.0
    mask[1, 0, HEAD_DIM:] = 1.0
    return jnp.asarray(onehot, BF16), jnp.asarray(mask, BF16)


def _moba_problem(qt_ref, k_ref, vt_ref, causal_ref, onehot_ref, mask_ref, o_ref, ka_ref, kb_ref, vta_ref, vtb_ref):
    t = MOBA_BLOCK
    vt_of_head, ones_rows = _values_with_ones(vt_ref, vta_ref, vtb_ref)
    s_len = k_ref.shape[0]
    nblk = s_len // t

    ka_ref[...] = k_ref[...] * mask_ref[0] + onehot_ref[0]
    kb_ref[...] = k_ref[...] * mask_ref[1] + onehot_ref[1]
    means = jnp.concatenate(
        [jnp.mean(k_ref[n * t:(n + 1) * t, :].astype(F32), axis=0, keepdims=True) for n in range(nblk)], axis=0)
    lane_m = lax.broadcasted_iota(jnp.int32, means.shape, 1)
    means2 = jnp.concatenate([jnp.where(lane_m < HEAD_DIM, means, 0.0),
                              jnp.where(lane_m >= HEAD_DIM, means, 0.0)], axis=0)

    hi = means2.astype(BF16)
    rest = means2 - hi.astype(F32)
    mid = rest.astype(BF16)
    lo = (rest - mid.astype(F32)).astype(BF16)
    means_split = jnp.concatenate([hi, mid, lo], axis=0)
    gates = {}

    def gate_scores(i):
        if i not in gates:
            g3 = jnp.dot(means_split, qt_ref[:, i * t:(i + 1) * t], preferred_element_type=F32)
            gates[i] = g3[:2 * nblk] + g3[2 * nblk:4 * nblk] + g3[4 * nblk:]
        return gates[i]

    zeros = jnp.zeros((HEAD_DIM - nblk, t), F32)
    k_of = (_key_blocks(ka_ref), _key_blocks(kb_ref))

    def make_query(i, h):
        qt = qt_ref[:, i * t:(i + 1) * t].astype(F32)
        if i <= MOBA_TOPK:
            return jnp.where(_row_masks(qt.shape)[h], qt, 0.0).astype(BF16)
        g = gate_scores(i)[h * nblk:(h + 1) * nblk]
        row = lax.broadcasted_iota(jnp.int32, g.shape, 0)
        rank = jnp.zeros(g.shape, F32)
        for kk in range(i):
            gk = g[kk:kk + 1, :]
            beats = (gk > g) | ((gk == g) & (row > kk))
            rank = rank + jnp.where(beats, 1.0, 0.0)
        drop = jnp.where((row < i) & (rank >= MOBA_TOPK), NEG, 0.0)
        parts = [qt[:HEAD_DIM], drop, zeros] if h == 0 else [drop, zeros, qt[HEAD_DIM:]]
        return jnp.concatenate(parts, axis=0).astype(BF16)

    def finish_tile(i, out_lo, out_hi):
        lo, _ = _row_masks(out_lo.shape)
        o_ref[i * t:(i + 1) * t, :] = jnp.where(lo, out_lo, out_hi).T.astype(BF16)

    return _Problem(make_query, lambda h: k_of[h], lambda i: (lambda j: causal_ref[...] if j == i else None),
                    vt_of_head, ones_rows, finish_tile)


def _moba_kernel(qt_ref, k_ref, vt_ref, causal_ref, onehot_ref, mask_ref, o_ref, ka_ref, kb_ref, vta_ref, vtb_ref,
                 s_refs, p_refs):
    problems = [_moba_problem(qt_ref.at[_tile_rows(u)], k_ref.at[u], vt_ref.at[_tile_rows(u)], causal_ref,
                              onehot_ref, mask_ref, o_ref.at[u], ka_ref.at[u], kb_ref.at[u], vta_ref.at[u],
                              vtb_ref.at[u]) for u in range(ATT_SUB)]
    _pipelined_attention(k_ref.shape[1] // ATT_TILE, problems, s_refs, p_refs)


def _moba_attention(kn, qvt, b, s, q_tile0, k_tile0, v_tile0, n_pairs):
    t = MOBA_BLOCK
    assert t == ATT_TILE and s // t <= SUBLANES
    onehot, mask = _moba_block_onehots(s)
    fixed3 = lambda b_, p: (0, 0, 0)
    return pl.pallas_call(
        _moba_kernel,
        grid=(b, n_pairs // ATT_SUB),
        in_specs=[_t_spec(s, q_tile0), _k_spec(s, k_tile0), _t_spec(s, v_tile0),
                  pl.BlockSpec((t, t), lambda b_, p: (0, 0)),
                  pl.BlockSpec(onehot.shape, fixed3), pl.BlockSpec(mask.shape, fixed3)],
        out_specs=_k_spec(s, 0),
        out_shape=jax.ShapeDtypeStruct((n_pairs, b * s, LANES), BF16),
        scratch_shapes=[pltpu.VMEM((ATT_SUB, s, LANES), BF16), pltpu.VMEM((ATT_SUB, s, LANES), BF16)]
        + _ones_value_scratch(s) + _attn_scratch(s),
        compiler_params=_cparams(("parallel", "parallel")),
        name="moba_attention",
    )(qvt, kn, qvt, _causal_table_t(t), onehot, mask)


def _diff_problem(qt_ref, k_ref, vt_ref, causal_ref, lam, g_ref, o_ref, lambda_init):
    t = ATT_TILE
    k_of = _key_blocks(k_ref)

    def make_query(i, h):
        qt = qt_ref[:, i * t:(i + 1) * t].astype(F32)
        return jnp.where(_row_masks(qt.shape)[h], qt, 0.0).astype(BF16)

    def finish_tile(i, out1, out2):
        o = out1 - lam * out2
        y = o * lax.rsqrt(jnp.mean(o * o, axis=0, keepdims=True) + NORM_EPS)
        o_ref[i * t:(i + 1) * t, :] = ((y.T * g_ref[...]) * (1.0 - lambda_init)).astype(BF16)

    return _Problem(make_query, lambda h: k_of, lambda i: (lambda j: causal_ref[...] if j == i else None),
                    lambda h: (lambda n: vt_ref[:, :n]), (None, None), finish_tile)


def _diff_kernel(qt_ref, k_ref, vt_ref, causal_ref, lq1_ref, lk1_ref, lq2_ref, lk2_ref, g_ref, o_ref,
                 s_refs, p_refs, *, lambda_init):
    lam = (jnp.exp(jnp.sum(lq1_ref[...] * lk1_ref[...], axis=-1, keepdims=True))
           - jnp.exp(jnp.sum(lq2_ref[...] * lk2_ref[...], axis=-1, keepdims=True)) + lambda_init)
    problems = [_diff_problem(qt_ref.at[_tile_rows(u)], k_ref.at[u], vt_ref.at[_tile_rows(u)], causal_ref, lam,
                              g_ref, o_ref.at[u], lambda_init) for u in range(ATT_SUB)]
    _pipelined_attention(k_ref.shape[1] // ATT_TILE, problems, s_refs, p_refs)


def _diff_attention(kn, qvt, b, s, lq1, lk1, lq2, lk2, subln_g, lambda_init, n_heads):
    t = ATT_TILE
    vec = lambda a: a.reshape(1, -1).astype(F32)
    small = lambda n: pl.BlockSpec((1, n), lambda b_, h: (0, 0))
    return pl.pallas_call(
        functools.partial(_diff_kernel, lambda_init=lambda_init),
        grid=(b, n_heads // ATT_SUB),
        in_specs=[_t_spec(s, 0), _k_spec(s, 0), _t_spec(s, n_heads), pl.BlockSpec((t, t), lambda b_, h: (0, 0)),
                  small(HEAD_DIM), small(HEAD_DIM), small(HEAD_DIM), small(HEAD_DIM), small(LANES)],
        out_specs=_k_spec(s, 0),
        out_shape=jax.ShapeDtypeStruct((n_heads, b * s, LANES), BF16),
        scratch_shapes=_attn_scratch(s),
        compiler_params=_cparams(("parallel", "parallel")),
        name="diff_attention",
    )(qvt, kn, qvt, _causal_table_t(t), vec(lq1), vec(lk1), vec(lq2), vec(lk2), vec(subln_g))


def _mixer_out_ffn_kernel(*refs, n_acts, chunk, final_norm):
    h_ref = refs[0]
    act_refs = refs[1:1 + n_acts]
    wo_ref, g_ref, wg_ref, wu_ref, wd_ref, gf_ref, o_ref = refs[1 + n_acts:]
    act = jnp.concatenate([a_ref[c] for a_ref in act_refs for c in range(a_ref.shape[0])], axis=1)
    x = h_ref[...] + jnp.dot(act, wo_ref[...], preferred_element_type=F32)
    ms = jnp.mean(x * x, axis=-1, keepdims=True)
    xn = (x * lax.rsqrt(ms + NORM_EPS) * g_ref[...]).astype(BF16)
    acc = x
    for c in range(wg_ref.shape[1] // chunk):
        sl = slice(c * chunk, (c + 1) * chunk)
        gate = jnp.dot(xn, wg_ref[:, sl], preferred_element_type=F32)
        up = jnp.dot(xn, wu_ref[:, sl], preferred_element_type=F32)
        mid = (gate * jax.nn.sigmoid(gate) * up).astype(BF16)
        acc = acc + jnp.dot(mid, wd_ref[sl, :], preferred_element_type=F32)
    if final_norm:
        ms = jnp.mean(acc * acc, axis=-1, keepdims=True)
        acc = acc * lax.rsqrt(ms + NORM_EPS) * gf_ref[...]
    o_ref[...] = acc


def _mixer_out_ffn(h, acts, w_out, g, layer, wg, wu, wd, g_final, final_norm):
    t, d = h.shape
    ff = wg.shape[2]
    tm = ROW_TILE
    row = lambda i: (i, 0)
    fixed = lambda i: (0, 0)
    of_layer = lambda i: (layer, 0, 0)
    once = dict(pipeline_mode=pl.Buffered(1))
    in_specs = [pl.BlockSpec((tm, d), row)]
    in_specs += [pl.BlockSpec((a.shape[0], tm, LANES), lambda i: (0, i, 0)) for a in acts]
    in_specs += [pl.BlockSpec(w_out.shape, fixed, **once), pl.BlockSpec((1, d), fixed),
                 pl.BlockSpec((None, d, ff), of_layer, **once), pl.BlockSpec((None, d, ff), of_layer, **once),
                 pl.BlockSpec((None, ff, d), of_layer, **once), pl.BlockSpec((1, d), fixed)]
    args = [h, *acts, w_out, g.reshape(1, d), wg, wu, wd, g_final.reshape(1, d)]
    return pl.pallas_call(
        functools.partial(_mixer_out_ffn_kernel, n_acts=len(acts), chunk=256, final_norm=final_norm),
        grid=(t // tm,),
        in_specs=in_specs,
        out_specs=pl.BlockSpec((tm, d), row),
        out_shape=jax.ShapeDtypeStruct((t, d), F32),
        compiler_params=_cparams(("parallel",)),
        name="mixer_out_ffn",
    )(*args)


def _cast_kernel(*refs):
    n = len(refs) // 2
    for src_ref, dst_ref in zip(refs[:n], refs[n:]):
        dst_ref[...] = src_ref[...].astype(BF16)


def _cast_bf16(arrays, n_chunks):
    specs = [pl.BlockSpec((1, a.shape[1] // n_chunks, a.shape[2]), lambda l, c: (l, c, 0)) for a in arrays]
    return pl.pallas_call(
        _cast_kernel,
        grid=(arrays[0].shape[0], n_chunks),
        in_specs=specs,
        out_specs=specs,
        out_shape=[jax.ShapeDtypeStruct(a.shape, BF16) for a in arrays],
        compiler_params=_cparams(("parallel", "parallel")),
        name="cast_weights",
    )(*arrays)


def _lambda_init(layer_idx):
    return 0.8 - 0.6 * math.exp(-0.3 * layer_idx)


def _split_weights(w_in, q_cols, k_cols, v_cols):
    cat = lambda cols: jnp.concatenate([w_in[:, a:b] for a, b in cols], axis=1)
    groups = [(cat([c]) * (ATTN_SCALE * LOG2E) if is_q else cat([c])) for c, is_q in
              sorted([(c, True) for c in q_cols] + [(c, False) for c in v_cols])]
    return cat(k_cols).astype(BF16), jnp.concatenate(groups, axis=1).T.astype(BF16)


def kernel(x, positions, ab_norm_g, ab_w_in, ab_w_out, diff_norm_g, diff_w_in, diff_w_out, diff_lambda_q1, diff_lambda_k1, diff_lambda_q2, diff_lambda_k2, diff_subln_g, ffn_norm_g, ffn_w_gate, ffn_w_up, ffn_w_down, final_norm_g):
    b, s, d = x.shape
    t = b * s
    n_pairs = d // (4 * HEAD_DIM)
    n_diff = d // (2 * HEAD_DIM)
    wa = n_pairs * LANES
    tabs = _rope_tables(positions)
    wg, wu, wd = _cast_bf16([ffn_w_gate, ffn_w_up, ffn_w_down], n_chunks=4)
    h = x.reshape(t, d)

    wk, wt = _split_weights(ab_w_in[0], q_cols=[(0, wa), (3 * wa, 4 * wa)],
                            k_cols=[(wa, 2 * wa), (4 * wa, 5 * wa)], v_cols=[(2 * wa, 3 * wa), (5 * wa, 6 * wa)])
    q_tiles0 = list(range(n_pairs)) + list(range(2 * n_pairs, 3 * n_pairs))
    kn, qvt = _norm_proj(h, ab_norm_g[0], wk, wt, tabs, q_tiles0)
    oa = _dilated_attention(kn, qvt, b, s, 0, 0, n_pairs, n_pairs)
    ob = _moba_attention(kn, qvt, b, s, 2 * n_pairs, n_pairs, 3 * n_pairs, n_pairs)
    h = _mixer_out_ffn(h, [oa, ob], ab_w_out[0].astype(BF16), ffn_norm_g[0], 0, wg, wu, wd, final_norm_g, False)

    wk, wt = _split_weights(diff_w_in[0], q_cols=[(0, d)], k_cols=[(d, 2 * d)], v_cols=[(2 * d, 3 * d)])
    kn, qvt = _norm_proj(h, diff_norm_g[0], wk, wt, tabs, list(range(n_diff)))
    od = _diff_attention(kn, qvt, b, s, diff_lambda_q1[0], diff_lambda_k1[0],
                         diff_lambda_q2[0], diff_lambda_k2[0], diff_subln_g[0], _lambda_init(1), n_diff)
    h = _mixer_out_ffn(h, [od], diff_w_out[0].astype(BF16), ffn_norm_g[1], 1, wg, wu, wd, final_norm_g, True)
    return h.reshape(b, s, d)
```

```python
import collections
import functools
import math

import numpy as np
import jax
import jax.numpy as jnp
from jax import lax
from jax.experimental import pallas as pl
from jax.experimental.pallas import tpu as pltpu

HEAD_DIM = 64
LANES = 128
SUBLANES = 8
ROPE_DIM = HEAD_DIM // 4
ROPE_HALF = ROPE_DIM // 2
ROPE_THETA = 500000.0
NORM_EPS = 1e-5
ATTN_SCALE = HEAD_DIM ** -0.5
LOG2E = math.log2(math.e)
NEG = -1e30
DILATED_CONFIGS = ((128, 1), (512, 4), (2048, 16))
MOBA_BLOCK = 256
MOBA_TOPK = 3

ATT_TILE = 256
ATT_SUB = 2
SETS = 2
ROW_TILE = 1024
FFN_COL_CHUNK = 256
PROJ_ROW_TILE = 1024
PROJ_COL_CHUNK = 512
ROPE_ROW_TILE = 1024
CAST_ROW_CHUNKS = 4
VMEM_LIMIT = 56 * 1024 * 1024

F32 = jnp.float32
BF16 = jnp.bfloat16
_NT = (((1,), (1,)), ((), ()))

assert ROPE_HALF == SUBLANES


def _cparams(sem):
    return pltpu.CompilerParams(dimension_semantics=sem, vmem_limit_bytes=VMEM_LIMIT)


def _rope_table_kernel(pos_ref, invf_ref, c_ref, s1_ref, s2_ref, ct_ref, st_ref):
    ang = invf_ref[...] * pos_ref[...].astype(F32)
    cos, sin = jnp.cos(ang), jnp.sin(ang)
    ct_ref[...] = cos
    st_ref[...] = sin
    tm = ang.shape[1]
    one = jnp.ones((HEAD_DIM - ROPE_DIM, tm), F32)
    zero = jnp.zeros((ROPE_HALF, tm), F32)
    zero_rest = jnp.zeros((HEAD_DIM - ROPE_DIM, tm), F32)
    c_ref[...] = jnp.concatenate([cos, cos, one] * 2, axis=0).T
    s1_ref[...] = jnp.concatenate([-sin, zero, zero_rest] * 2, axis=0).T
    s2_ref[...] = jnp.concatenate([zero, sin, zero_rest] * 2, axis=0).T


def _rope_tables(positions):
    t = positions.size
    tm = ROPE_ROW_TILE
    inv_freq = ROPE_THETA ** (-jnp.arange(0, ROPE_DIM, 2, dtype=F32) / ROPE_DIM)
    tab = jax.ShapeDtypeStruct((t, LANES), F32)
    tab_t = jax.ShapeDtypeStruct((ROPE_HALF, t), F32)
    row = lambda i: (i, 0)
    col = lambda i: (0, i)
    return pl.pallas_call(
        _rope_table_kernel,
        grid=(t // tm,),
        in_specs=[pl.BlockSpec((1, tm), col), pl.BlockSpec((ROPE_HALF, 1), lambda i: (0, 0))],
        out_specs=[pl.BlockSpec((tm, LANES), row)] * 3 + [pl.BlockSpec((ROPE_HALF, tm), col)] * 2,
        out_shape=[tab] * 3 + [tab_t] * 2,
        compiler_params=_cparams(("parallel",)),
        name="rope_tables",
    )(positions.reshape(1, t), inv_freq.reshape(ROPE_HALF, 1))


def _norm_proj_kernel(x_ref, g_ref, wk_ref, wt_ref, c_ref, s1_ref, s2_ref, ct_ref, st_ref,
                      k_ref, qvt_ref, acc_ref, *, q_tiles):
    x = x_ref[...]
    ms = jnp.mean(x * x, axis=-1, keepdims=True)
    xn = (x * lax.rsqrt(ms + NORM_EPS) * g_ref[...]).astype(BF16)

    c, s1, s2 = c_ref[...], s1_ref[...], s2_ref[...]
    chunk = PROJ_COL_CHUNK
    for n in range(wk_ref.shape[1] // chunk):
        acc = jnp.dot(xn, wk_ref[:, n * chunk:(n + 1) * chunk], preferred_element_type=F32)
        for t in range(chunk // LANES):
            seg = acc[:, t * LANES:(t + 1) * LANES]
            seg = seg * c + pltpu.roll(seg, LANES - ROPE_HALF, 1) * s1 + pltpu.roll(seg, ROPE_HALF, 1) * s2
            k_ref[n * (chunk // LANES) + t] = seg.astype(BF16)

    acc_ref[...] = lax.dot_general(wt_ref[...], xn, _NT, preferred_element_type=F32)
    ct, st = ct_ref[...], st_ref[...]
    for r in range(qvt_ref.shape[0] // LANES):
        if r not in q_tiles:
            qvt_ref[r * LANES:(r + 1) * LANES, :] = acc_ref[r * LANES:(r + 1) * LANES, :].astype(BF16)
            continue
        for r0 in range(r * LANES, (r + 1) * LANES, HEAD_DIM):
            x1 = acc_ref[r0:r0 + ROPE_HALF, :]
            x2 = acc_ref[r0 + ROPE_HALF:r0 + ROPE_DIM, :]
            rot = jnp.concatenate([x1 * ct - x2 * st, x2 * ct + x1 * st], axis=0)
            qvt_ref[r0:r0 + ROPE_DIM, :] = rot.astype(BF16)
            qvt_ref[r0 + ROPE_DIM:r0 + HEAD_DIM, :] = acc_ref[r0 + ROPE_DIM:r0 + HEAD_DIM, :].astype(BF16)


def _norm_proj(x, g, wk, wt, tabs, q_tiles):
    t, d = x.shape
    nk, nt = wk.shape[1], wt.shape[0]
    tm = PROJ_ROW_TILE
    row = lambda i: (i, 0)
    col = lambda i: (0, i)
    fixed = lambda i: (0, 0)
    once = dict(pipeline_mode=pl.Buffered(1))
    return pl.pallas_call(
        functools.partial(_norm_proj_kernel, q_tiles=frozenset(q_tiles)),
        grid=(t // tm,),
        in_specs=[pl.BlockSpec((tm, d), row), pl.BlockSpec((1, d), fixed),
                  pl.BlockSpec((d, nk), fixed, **once), pl.BlockSpec((nt, d), fixed, **once),
                  pl.BlockSpec((tm, LANES), row), pl.BlockSpec((tm, LANES), row), pl.BlockSpec((tm, LANES), row),
                  pl.BlockSpec((ROPE_HALF, tm), col), pl.BlockSpec((ROPE_HALF, tm), col)],
        out_specs=[pl.BlockSpec((nk // LANES, tm, LANES), lambda i: (0, i, 0)), pl.BlockSpec((nt, tm), col)],
        out_shape=[jax.ShapeDtypeStruct((nk // LANES, t, LANES), BF16), jax.ShapeDtypeStruct((nt, t), BF16)],
        scratch_shapes=[pltpu.VMEM((nt, tm), F32)],
        compiler_params=_cparams(("parallel",)),
        name="norm_proj_rope",
    )(x, g.reshape(1, d), wk, wt, *tabs)


def _row_masks(shape):
    row = lax.broadcasted_iota(jnp.int32, shape, 0)
    return row < HEAD_DIM, row >= HEAD_DIM


def _causal_table_t(t):
    d = np.arange(t)[None, :] - np.arange(t)[:, None]
    return jnp.asarray(_check_diagonal_table(np.where(d >= 0, 0.0, NEG)), F32)


def _check_diagonal_table(table):
    half = table.shape[0] // 2
    assert (table[half:, :half] == NEG).all()
    return table


def _logits_stage(qt, k_of, bias_of, n_blocks, s_ref):
    t = ATT_TILE
    hb = t // 2
    m = None
    for j in range(n_blocks - 1):
        s = jnp.dot(k_of(j), qt, preferred_element_type=F32)
        bias = bias_of(j)
        if bias is not None:
            s = s + bias
        s_ref[j * t:(j + 1) * t, :] = s
        mj = jnp.max(s, axis=0, keepdims=True)
        m = mj if m is None else jnp.maximum(m, mj)
    j = n_blocks - 1
    s = jnp.dot(k_of(j), qt, preferred_element_type=F32)
    bias = bias_of(j)
    top = s[:hb] + bias[:hb]
    bot = s[hb:, hb:] + bias[hb:, hb:]
    s_ref[j * t:j * t + hb, :] = top
    s_ref[j * t + hb:(j + 1) * t, :] = jnp.concatenate([bot, bot], axis=1)
    m_top = jnp.max(top, axis=0, keepdims=True)
    m_bot = jnp.max(bot, axis=0, keepdims=True)
    mj = jnp.concatenate([m_top[:, :hb], jnp.maximum(m_top[:, hb:], m_bot)], axis=1)
    return mj if m is None else jnp.maximum(m, mj)


def _probs_stage(m, n_blocks, s_ref, p_ref, want_sum):
    t = ATT_TILE
    hb = t // 2
    l = None
    for j in range(n_blocks - 1):
        p = jnp.exp2(s_ref[j * t:(j + 1) * t, :] - m)
        p_ref[j * t:(j + 1) * t, :] = p.astype(BF16)
        if want_sum:
            lj = jnp.sum(p, axis=0, keepdims=True)
            l = lj if l is None else l + lj
    j = n_blocks - 1
    p_top = jnp.exp2(s_ref[j * t:j * t + hb, :] - m)
    p_bot = jnp.exp2(s_ref[j * t + hb:(j + 1) * t, hb:] - m[:, hb:])
    p_ref[j * t:j * t + hb, :] = p_top.astype(BF16)
    p_ref[j * t + hb:(j + 1) * t, :] = jnp.concatenate([jnp.zeros_like(p_bot), p_bot], axis=1).astype(BF16)
    if want_sum:
        l_top = jnp.sum(p_top, axis=0, keepdims=True)
        l_bot = jnp.sum(p_bot, axis=0, keepdims=True)
        lj = jnp.concatenate([l_top[:, :hb], l_top[:, hb:] + l_bot], axis=1)
        l = lj if l is None else l + lj
    return l


def _values_stage(l, vt_of, n_blocks, p_ref, ones_row):
    n = n_blocks * ATT_TILE
    acc = jnp.dot(vt_of(n), p_ref[:n, :], preferred_element_type=F32)
    if l is None:
        l = acc[ones_row:ones_row + 1, :]
    return acc * (1.0 / l)


_Problem = collections.namedtuple(
    "_Problem", "make_query k_of_head bias_of_tile vt_of_head ones_rows finish_tile")


def _pipelined_attention(n_tiles, problems, s_refs, p_refs):
    items = [(pr, i, h) for i in reversed(range(n_tiles)) for pr in problems for h in range(2)]
    n = len(items)
    ms, ls, outs = {}, {}, {}
    for step in range(n + 2):
        if 1 <= step <= n:
            c = step - 1
            pr, i, h = items[c]
            ls[c] = _probs_stage(ms.pop(c), i + 1, s_refs.at[c % SETS], p_refs.at[c % SETS],
                                 pr.ones_rows[h] is None)
        if step < n:
            pr, i, h = items[step]
            ms[step] = _logits_stage(pr.make_query(i, h), pr.k_of_head(h), pr.bias_of_tile(i), i + 1,
                                     s_refs.at[step % SETS])
        if 2 <= step <= n + 1:
            c = step - 2
            pr, i, h = items[c]
            outs[c] = _values_stage(ls.pop(c), pr.vt_of_head(h), i + 1, p_refs.at[c % SETS], pr.ones_rows[h])
            if h == 1:
                pr.finish_tile(i, outs.pop(c - 1), outs.pop(c))


def _values_with_ones(vt_ref, vta_ref, vtb_ref):
    ones = jnp.ones((HEAD_DIM, vt_ref.shape[1]), BF16)
    vta_ref[:HEAD_DIM, :] = vt_ref[:HEAD_DIM, :]
    vta_ref[HEAD_DIM:, :] = ones
    vtb_ref[:HEAD_DIM, :] = ones
    vtb_ref[HEAD_DIM:, :] = vt_ref[HEAD_DIM:, :]
    vt_of = (lambda n: vta_ref[:, :n], lambda n: vtb_ref[:, :n])
    return (lambda h: vt_of[h]), (HEAD_DIM, 0)


def _key_blocks(ref):
    t = ATT_TILE
    return lambda j: ref[j * t:(j + 1) * t, :]


def _tile_rows(u):
    return pl.ds(u * LANES, LANES)


def _attn_scratch(s):
    return [pltpu.VMEM((SETS, s, ATT_TILE), F32), pltpu.VMEM((SETS, s, ATT_TILE), BF16)]


def _k_spec(s, tile0):
    assert tile0 % ATT_SUB == 0
    return pl.BlockSpec((ATT_SUB, s, LANES), lambda b, p: (tile0 // ATT_SUB + p, b, 0))


def _t_spec(s, tile0):
    assert tile0 % ATT_SUB == 0
    return pl.BlockSpec((ATT_SUB * LANES, s), lambda b, p: (tile0 // ATT_SUB + p, b))


def _dilated_bias_tables_t(t):
    windows = sorted(w for w, _ in DILATED_CONFIGS)
    far = windows[-2] // t + 1
    tabs = []
    for delta in range(far + 1):
        d = delta * t + np.arange(t)[None, :] - np.arange(t)[:, None]
        mult = np.zeros((t, t), np.float64)
        for window, dil in DILATED_CONFIGS:
            mult += (d >= 0) & (d % dil == 0) & (d <= window)
        tabs.append(np.where(mult > 0, np.log2(np.maximum(mult, 1.0)), NEG))
    _check_diagonal_table(tabs[0])
    return jnp.asarray(np.stack(tabs), F32)


def _dilated_problem(qt_ref, k_ref, vt_ref, bias_ref, o_ref, vta_ref, vtb_ref):
    t = ATT_TILE
    n_tab = bias_ref.shape[0]
    k_of = _key_blocks(k_ref)
    vt_of_head, ones_rows = _values_with_ones(vt_ref, vta_ref, vtb_ref)

    def make_query(i, h):
        qt = qt_ref[:, i * t:(i + 1) * t].astype(F32)
        return jnp.where(_row_masks(qt.shape)[h], qt, 0.0).astype(BF16)

    def finish_tile(i, out_lo, out_hi):
        lo, _ = _row_masks(out_lo.shape)
        o_ref[i * t:(i + 1) * t, :] = jnp.where(lo, out_lo, out_hi).T.astype(BF16)

    return _Problem(make_query, lambda h: k_of, lambda i: (lambda j: bias_ref[min(i - j, n_tab - 1)]),
                    vt_of_head, ones_rows, finish_tile)


def _dilated_kernel(qt_ref, k_ref, vt_ref, bias_ref, o_ref, vta_ref, vtb_ref, s_refs, p_refs):
    problems = [_dilated_problem(qt_ref.at[_tile_rows(u)], k_ref.at[u], vt_ref.at[_tile_rows(u)], bias_ref,
                                 o_ref.at[u], vta_ref.at[u], vtb_ref.at[u]) for u in range(ATT_SUB)]
    _pipelined_attention(k_ref.shape[1] // ATT_TILE, problems, s_refs, p_refs)


def _ones_value_scratch(s):
    return [pltpu.VMEM((ATT_SUB, LANES, s), BF16), pltpu.VMEM((ATT_SUB, LANES, s), BF16)]


def _dilated_attention(kn, qvt, b, s, q_tile0, k_tile0, v_tile0, n_pairs):
    bias = _dilated_bias_tables_t(ATT_TILE)
    return pl.pallas_call(
        _dilated_kernel,
        grid=(b, n_pairs // ATT_SUB),
        in_specs=[_t_spec(s, q_tile0), _k_spec(s, k_tile0), _t_spec(s, v_tile0),
                  pl.BlockSpec(bias.shape, lambda b_, p: (0, 0, 0))],
        out_specs=_k_spec(s, 0),
        out_shape=jax.ShapeDtypeStruct((n_pairs, b * s, LANES), BF16),
        scratch_shapes=_ones_value_scratch(s) + _attn_scratch(s),
        compiler_params=_cparams(("parallel", "parallel")),
        name="dilated_attention",
    )(qvt, kn, qvt, bias)


def _moba_block_onehots(s):
    blk = np.arange(s) // MOBA_BLOCK
    onehot = np.zeros((2, s, LANES), np.float32)
    onehot[0, np.arange(s), HEAD_DIM + blk] = 1.0
    onehot[1, np.arange(s), blk] = 1.0
    mask = np.zeros((2, 1, LANES), np.float32)
    mask[0, 0, :HEAD_DIM] = 1.0
    mask[1, 0, HEAD_DIM:] = 1.0
    return jnp.asarray(onehot, BF16), jnp.asarray(mask, BF16)


def _moba_problem(qt_ref, k_ref, vt_ref, causal_ref, onehot_ref, mask_ref, o_ref, ka_ref, kb_ref, vta_ref, vtb_ref):
    t = MOBA_BLOCK
    vt_of_head, ones_rows = _values_with_ones(vt_ref, vta_ref, vtb_ref)
    s_len = k_ref.shape[0]
    nblk = s_len // t

    ka_ref[...] = k_ref[...] * mask_ref[0] + onehot_ref[0]
    kb_ref[...] = k_ref[...] * mask_ref[1] + onehot_ref[1]
    means = jnp.concatenate(
        [jnp.mean(k_ref[n * t:(n + 1) * t, :].astype(F32), axis=0, keepdims=True) for n in range(nblk)], axis=0)
    lane_m = lax.broadcasted_iota(jnp.int32, means.shape, 1)
    means2 = jnp.concatenate([jnp.where(lane_m < HEAD_DIM, means, 0.0),
                              jnp.where(lane_m >= HEAD_DIM, means, 0.0)], axis=0)

    hi = means2.astype(BF16)
    rest = means2 - hi.astype(F32)
    mid = rest.astype(BF16)
    lo = (rest - mid.astype(F32)).astype(BF16)
    means_split = jnp.concatenate([hi, mid, lo], axis=0)
    gates = {}

    def gate_scores(i):
        if i not in gates:
            g3 = jnp.dot(means_split, qt_ref[:, i * t:(i + 1) * t], preferred_element_type=F32)
            gates[i] = g3[:2 * nblk] + g3[2 * nblk:4 * nblk] + g3[4 * nblk:]
        return gates[i]

    zeros = jnp.zeros((HEAD_DIM - nblk, t), F32)
    k_of = (_key_blocks(ka_ref), _key_blocks(kb_ref))

    def make_query(i, h):
        qt = qt_ref[:, i * t:(i + 1) * t].astype(F32)
        if i <= MOBA_TOPK:
            return jnp.where(_row_masks(qt.shape)[h], qt, 0.0).astype(BF16)
        g = gate_scores(i)[h * nblk:(h + 1) * nblk]
        row = lax.broadcasted_iota(jnp.int32, g.shape, 0)
        rank = jnp.zeros(g.shape, F32)
        for kk in range(i):
            gk = g[kk:kk + 1, :]
            beats = (gk > g) | ((gk == g) & (row > kk))
            rank = rank + jnp.where(beats, 1.0, 0.0)
        drop = jnp.where((row < i) & (rank >= MOBA_TOPK), NEG, 0.0)
        parts = [qt[:HEAD_DIM], drop, zeros] if h == 0 else [drop, zeros, qt[HEAD_DIM:]]
        return jnp.concatenate(parts, axis=0).astype(BF16)

    def finish_tile(i, out_lo, out_hi):
        lo, _ = _row_masks(out_lo.shape)
        o_ref[i * t:(i + 1) * t, :] = jnp.where(lo, out_lo, out_hi).T.astype(BF16)

    return _Problem(make_query, lambda h: k_of[h], lambda i: (lambda j: causal_ref[...] if j == i else None),
                    vt_of_head, ones_rows, finish_tile)


def _moba_kernel(qt_ref, k_ref, vt_ref, causal_ref, onehot_ref, mask_ref, o_ref, ka_ref, kb_ref, vta_ref, vtb_ref,
                 s_refs, p_refs):
    problems = [_moba_problem(qt_ref.at[_tile_rows(u)], k_ref.at[u], vt_ref.at[_tile_rows(u)], causal_ref,
                              onehot_ref, mask_ref, o_ref.at[u], ka_ref.at[u], kb_ref.at[u], vta_ref.at[u],
                              vtb_ref.at[u]) for u in range(ATT_SUB)]
    _pipelined_attention(k_ref.shape[1] // ATT_TILE, problems, s_refs, p_refs)


def _moba_attention(kn, qvt, b, s, q_tile0, k_tile0, v_tile0, n_pairs):
    t = MOBA_BLOCK
    assert t == ATT_TILE and s // t <= SUBLANES
    onehot, mask = _moba_block_onehots(s)
    fixed3 = lambda b_, p: (0, 0, 0)
    return pl.pallas_call(
        _moba_kernel,
        grid=(b, n_pairs // ATT_SUB),
        in_specs=[_t_spec(s, q_tile0), _k_spec(s, k_tile0), _t_spec(s, v_tile0),
                  pl.BlockSpec((t, t), lambda b_, p: (0, 0)),
                  pl.BlockSpec(onehot.shape, fixed3), pl.BlockSpec(mask.shape, fixed3)],
        out_specs=_k_spec(s, 0),
        out_shape=jax.ShapeDtypeStruct((n_pairs, b * s, LANES), BF16),
        scratch_shapes=[pltpu.VMEM((ATT_SUB, s, LANES), BF16), pltpu.VMEM((ATT_SUB, s, LANES), BF16)]
        + _ones_value_scratch(s) + _attn_scratch(s),
        compiler_params=_cparams(("parallel", "parallel")),
        name="moba_attention",
    )(qvt, kn, qvt, _causal_table_t(t), onehot, mask)


def _diff_problem(qt_ref, k_ref, vt_ref, causal_ref, lam, g_ref, o_ref, lambda_init):
    t = ATT_TILE
    k_of = _key_blocks(k_ref)

    def make_query(i, h):
        qt = qt_ref[:, i * t:(i + 1) * t].astype(F32)
        return jnp.where(_row_masks(qt.shape)[h], qt, 0.0).astype(BF16)

    def finish_tile(i, out1, out2):
        o = out1 - lam * out2
        y = o * lax.rsqrt(jnp.mean(o * o, axis=0, keepdims=True) + NORM_EPS)
        o_ref[i * t:(i + 1) * t, :] = ((y.T * g_ref[...]) * (1.0 - lambda_init)).astype(BF16)

    return _Problem(make_query, lambda h: k_of, lambda i: (lambda j: causal_ref[...] if j == i else None),
                    lambda h: (lambda n: vt_ref[:, :n]), (None, None), finish_tile)


def _diff_kernel(qt_ref, k_ref, vt_ref, causal_ref, lq1_ref, lk1_ref, lq2_ref, lk2_ref, g_ref, o_ref,
                 s_refs, p_refs, *, lambda_init):
    lam = (jnp.exp(jnp.sum(lq1_ref[...] * lk1_ref[...], axis=-1, keepdims=True))
           - jnp.exp(jnp.sum(lq2_ref[...] * lk2_ref[...], axis=-1, keepdims=True)) + lambda_init)
    problems = [_diff_problem(qt_ref.at[_tile_rows(u)], k_ref.at[u], vt_ref.at[_tile_rows(u)], causal_ref, lam,
                              g_ref, o_ref.at[u], lambda_init) for u in range(ATT_SUB)]
    _pipelined_attention(k_ref.shape[1] // ATT_TILE, problems, s_refs, p_refs)


def _diff_attention(kn, qvt, b, s, lq1, lk1, lq2, lk2, subln_g, lambda_init, n_heads):
    t = ATT_TILE
    vec = lambda a: a.reshape(1, -1).astype(F32)
    small = lambda n: pl.BlockSpec((1, n), lambda b_, h: (0, 0))
    return pl.pallas_call(
        functools.partial(_diff_kernel, lambda_init=lambda_init),
        grid=(b, n_heads // ATT_SUB),
        in_specs=[_t_spec(s, 0), _k_spec(s, 0), _t_spec(s, n_heads), pl.BlockSpec((t, t), lambda b_, h: (0, 0)),
                  small(HEAD_DIM), small(HEAD_DIM), small(HEAD_DIM), small(HEAD_DIM), small(LANES)],
        out_specs=_k_spec(s, 0),
        out_shape=jax.ShapeDtypeStruct((n_heads, b * s, LANES), BF16),
        scratch_shapes=_attn_scratch(s),
        compiler_params=_cparams(("parallel", "parallel")),
        name="diff_attention",
    )(qvt, kn, qvt, _causal_table_t(t), vec(lq1), vec(lk1), vec(lq2), vec(lk2), vec(subln_g))


def _mixer_out_ffn_kernel(*refs, n_acts, chunk, final_norm):
    h_ref = refs[0]
    act_refs = refs[1:1 + n_acts]
    wo_ref, g_ref, wg_ref, wu_ref, wd_ref, gf_ref, o_ref = refs[1 + n_acts:]
    act = jnp.concatenate([a_ref[c] for a_ref in act_refs for c in range(a_ref.shape[0])], axis=1)
    x = h_ref[...] + jnp.dot(act, wo_ref[...], preferred_element_type=F32)
    ms = jnp.mean(x * x, axis=-1, keepdims=True)
    xn = (x * lax.rsqrt(ms + NORM_EPS) * g_ref[...]).astype(BF16)
    acc = x
    for c in range(wg_ref.shape[1] // chunk):
        sl = slice(c * chunk, (c + 1) * chunk)
        gate = jnp.dot(xn, wg_ref[:, sl], preferred_element_type=F32)
        up = jnp.dot(xn, wu_ref[:, sl], preferred_element_type=F32)
        mid = (gate * jax.nn.sigmoid(gate) * up).astype(BF16)
        acc = acc + jnp.dot(mid, wd_ref[sl, :], preferred_element_type=F32)
    if final_norm:
        ms = jnp.mean(acc * acc, axis=-1, keepdims=True)
        acc = acc * lax.rsqrt(ms + NORM_EPS) * gf_ref[...]
    o_ref[...] = acc


def _mixer_out_ffn(h, acts, w_out, g, layer, wg, wu, wd, g_final, final_norm):
    t, d = h.shape
    ff = wg.shape[2]
    tm = ROW_TILE
    row = lambda i: (i, 0)
    fixed = lambda i: (0, 0)
    of_layer = lambda i: (layer, 0, 0)
    once = dict(pipeline_mode=pl.Buffered(1))
    in_specs = [pl.BlockSpec((tm, d), row)]
    in_specs += [pl.BlockSpec((a.shape[0], tm, LANES), lambda i: (0, i, 0)) for a in acts]
    in_specs += [pl.BlockSpec(w_out.shape, fixed, **once), pl.BlockSpec((1, d), fixed),
                 pl.BlockSpec((None, d, ff), of_layer, **once), pl.BlockSpec((None, d, ff), of_layer, **once),
                 pl.BlockSpec((None, ff, d), of_layer, **once), pl.BlockSpec((1, d), fixed)]
    args = [h, *acts, w_out, g.reshape(1, d), wg, wu, wd, g_final.reshape(1, d)]
    return pl.pallas_call(
        functools.partial(_mixer_out_ffn_kernel, n_acts=len(acts), chunk=FFN_COL_CHUNK, final_norm=final_norm),
        grid=(t // tm,),
        in_specs=in_specs,
        out_specs=pl.BlockSpec((tm, d), row),
        out_shape=jax.ShapeDtypeStruct((t, d), F32),
        compiler_params=_cparams(("parallel",)),
        name="mixer_out_ffn",
    )(*args)


def _cast_kernel(*refs):
    n = len(refs) // 2
    for src_ref, dst_ref in zip(refs[:n], refs[n:]):
        dst_ref[...] = src_ref[...].astype(BF16)


def _cast_bf16(arrays, n_chunks):
    specs = [pl.BlockSpec((1, a.shape[1] // n_chunks, a.shape[2]), lambda l, c: (l, c, 0)) for a in arrays]
    return pl.pallas_call(
        _cast_kernel,
        grid=(arrays[0].shape[0], n_chunks),
        in_specs=specs,
        out_specs=specs,
        out_shape=[jax.ShapeDtypeStruct(a.shape, BF16) for a in arrays],
        compiler_params=_cparams(("parallel", "parallel")),
        name="cast_weights",
    )(*arrays)


def _lambda_init(layer_idx):
    return 0.8 - 0.6 * math.exp(-0.3 * layer_idx)


def _split_weights(w_in, q_cols, k_cols, v_cols):
    cat = lambda cols: jnp.concatenate([w_in[:, a:b] for a, b in cols], axis=1)
    groups = [(cat([c]) * (ATTN_SCALE * LOG2E) if is_q else cat([c])) for c, is_q in
              sorted([(c, True) for c in q_cols] + [(c, False) for c in v_cols])]
    return cat(k_cols).astype(BF16), jnp.concatenate(groups, axis=1).T.astype(BF16)


def kernel(x, positions, ab_norm_g, ab_w_in, ab_w_out, diff_norm_g, diff_w_in, diff_w_out, diff_lambda_q1, diff_lambda_k1, diff_lambda_q2, diff_lambda_k2, diff_subln_g, ffn_norm_g, ffn_w_gate, ffn_w_up, ffn_w_down, final_norm_g):
    b, s, d = x.shape
    t = b * s
    n_pairs = d // (4 * HEAD_DIM)
    n_diff = d // (2 * HEAD_DIM)
    wa = n_pairs * LANES
    tabs = _rope_tables(positions)
    wg, wu, wd = _cast_bf16([ffn_w_gate, ffn_w_up, ffn_w_down], n_chunks=CAST_ROW_CHUNKS)
    h = x.reshape(t, d)

    wk, wt = _split_weights(ab_w_in[0], q_cols=[(0, wa), (3 * wa, 4 * wa)],
                            k_cols=[(wa, 2 * wa), (4 * wa, 5 * wa)], v_cols=[(2 * wa, 3 * wa), (5 * wa, 6 * wa)])
    q_tiles0 = list(range(n_pairs)) + list(range(2 * n_pairs, 3 * n_pairs))
    kn, qvt = _norm_proj(h, ab_norm_g[0], wk, wt, tabs, q_tiles0)
    oa = _dilated_attention(kn, qvt, b, s, 0, 0, n_pairs, n_pairs)
    ob = _moba_attention(kn, qvt, b, s, 2 * n_pairs, n_pairs, 3 * n_pairs, n_pairs)
    h = _mixer_out_ffn(h, [oa, ob], ab_w_out[0].astype(BF16), ffn_norm_g[0], 0, wg, wu, wd, final_norm_g, False)

    wk, wt = _split_weights(diff_w_in[0], q_cols=[(0, d)], k_cols=[(d, 2 * d)], v_cols=[(2 * d, 3 * d)])
    kn, qvt = _norm_proj(h, diff_norm_g[0], wk, wt, tabs, list(range(n_diff)))
    od = _diff_attention(kn, qvt, b, s, diff_lambda_q1[0], diff_lambda_k1[0],
                         diff_lambda_q2[0], diff_lambda_k2[0], diff_subln_g[0], _lambda_init(1), n_diff)
    h = _mixer_out_ffn(h, [od], diff_w_out[0].astype(BF16), ffn_norm_g[1], 1, wg, wu, wd, final_norm_g, True)
    return h.reshape(b, s, d)
```
